```python
import math
import jax, jax.numpy as jnp
from jax import lax
import numpy as np

D_MODEL = 4096
BATCH = 4
SEQ = 2048
DEPTH = 2
DEC_BATCH = 128
DEC_SEQ = 1
PAST_LEN = 16384
PAGE_SIZE = 128

N_HEADS_A = 16
HEAD_DIM_A = 128
BLOCK_A = 256
TOP_K_A = 3
Q_CHUNK_A = 16
N_HEADS_B = 16
NOPE_DIM = 128
ROPE_DIM = 32
V_DIM_B = 128
Q_LORA = 1024
KV_LORA = 256
ROPE_THETA = 10000.0
ATTN_Q_BLOCK = 128
D_FF = 4 * D_MODEL
ALPHA = (2 * DEPTH) ** 0.25
BETA = (8 * DEPTH) ** -0.25
LN_EPS = 1e-5
RMS_EPS = 1e-6
N_IN = N_HEADS_A * HEAD_DIM_A + 2 * HEAD_DIM_A + Q_LORA + KV_LORA + ROPE_DIM + 2 * D_MODEL

kernel_name = 'hybrid_moba_mla_deepnorm_step'


def _layernorm(x, g, b):
    xf = x.astype(jnp.float32)
    mu = jnp.mean(xf, -1, keepdims=True)
    var = jnp.mean(jnp.square(xf - mu), -1, keepdims=True)
    return ((xf - mu) * lax.rsqrt(var + LN_EPS) * g + b).astype(x.dtype)


def _rmsnorm(x, g):
    xf = x.astype(jnp.float32)
    return (xf * lax.rsqrt(jnp.mean(xf * xf, -1, keepdims=True) + RMS_EPS) * g).astype(x.dtype)


def _rope(x, pos):
    half = ROPE_DIM // 2
    inv = jnp.power(ROPE_THETA, -2.0 * jnp.arange(half, dtype=jnp.float32) / ROPE_DIM)
    ang = pos.astype(jnp.float32)[:, None] * inv[None, :]
    shape = (1, pos.shape[0]) + (1,) * (x.ndim - 3) + (half,)
    cos = jnp.cos(ang).reshape(shape)
    sin = jnp.sin(ang).reshape(shape)
    xf = x.astype(jnp.float32)
    x1, x2 = xf[..., :half], xf[..., half:]
    return jnp.concatenate([x1 * cos - x2 * sin, x2 * cos + x1 * sin], -1).astype(x.dtype)


def _alibi_slopes():
    return jnp.exp2(-8.0 * jnp.arange(1, N_HEADS_A + 1, dtype=jnp.float32) / N_HEADS_A)


def _project(x, pos, w_in, w_uq, g_qn, g_kvn, w_uk):
    b, s, _ = x.shape
    h = x @ w_in
    sizes = [N_HEADS_A * HEAD_DIM_A, HEAD_DIM_A, HEAD_DIM_A, Q_LORA, KV_LORA, ROPE_DIM, D_MODEL, D_MODEL]
    cuts = [int(c) for c in np.cumsum(sizes)[:-1]]
    q_a, k_a, v_a, c_q, c_kv, k_r, g_a, g_b = jnp.split(h, cuts, axis=-1)
    q_a = q_a.reshape(b, s, N_HEADS_A, HEAD_DIM_A)
    q_b = (_rmsnorm(c_q, g_qn) @ w_uq).reshape(b, s, N_HEADS_B, NOPE_DIM + ROPE_DIM)
    q_rope = _rope(q_b[..., NOPE_DIM:], pos)
    q_lat = jnp.einsum('bshn,chn->bshc', q_b[..., :NOPE_DIM], w_uk)
    ckv = _rmsnorm(c_kv, g_kvn)
    krope = _rope(k_r, pos)
    return q_a, k_a, v_a, q_lat, q_rope, ckv, krope, jax.nn.sigmoid(g_a), jax.nn.sigmoid(g_b)


def _moba_core(q, qpos, k_sel, v_sel, sel_pos, sel_valid, k_own, v_own, own_pos, slopes):
    scale = HEAD_DIM_A ** -0.5
    s_own = jnp.einsum('bqhd,bjd->bqhj', q, k_own).astype(jnp.float32) * scale
    dist_own = (qpos[:, None] - own_pos[None, :]).astype(jnp.float32)
    s_own = s_own - slopes[None, None, :, None] * dist_own[None, :, None, :]
    s_own = jnp.where((own_pos[None, :] <= qpos[:, None])[None, :, None, :], s_own, -jnp.inf)
    if k_sel is None:
        p = jax.nn.softmax(s_own, axis=-1).astype(v_own.dtype)
        return jnp.einsum('bqhj,bjd->bqhd', p, v_own)
    s_sel = jnp.einsum('bqhd,bqhkjd->bqhkj', q, k_sel).astype(jnp.float32) * scale
    dist_sel = (qpos[None, :, None, None, None] - sel_pos).astype(jnp.float32)
    s_sel = s_sel - slopes[None, None, :, None, None] * dist_sel
    s_sel = jnp.where(sel_valid[..., None], s_sel, -jnp.inf)
    bq, nq, nh, nk, nl = s_sel.shape
    s_all = jnp.concatenate([s_sel.reshape(bq, nq, nh, nk * nl), s_own], axis=-1)
    p = jax.nn.softmax(s_all, axis=-1).astype(v_own.dtype)
    p_sel = p[..., :nk * nl].reshape(s_sel.shape)
    p_own = p[..., nk * nl:]
    return jnp.einsum('bqhkj,bqhkjd->bqhd', p_sel, v_sel) + jnp.einsum('bqhj,bjd->bqhd', p_own, v_own)


def _moba_prompt(q, k, v, slopes):
    b, s, h, d = q.shape
    nb = -(-s // BLOCK_A)
    s_pad = nb * BLOCK_A
    kp = jnp.pad(k, ((0, 0), (0, s_pad - s), (0, 0)))
    vp = jnp.pad(v, ((0, 0), (0, s_pad - s), (0, 0)))
    k_blk = kp.reshape(b, nb, BLOCK_A, d)
    v_blk = vp.reshape(b, nb, BLOCK_A, d)
    n_top = min(TOP_K_A, nb - 1)
    blk_of = jnp.arange(s) // BLOCK_A
    if n_top > 0:
        k_mean = jnp.mean(k_blk.astype(jnp.float32), axis=2)
        gate = jnp.einsum('bshd,bnd->bshn', q.astype(jnp.float32), k_mean)
        past = jnp.arange(nb)[None, :] < blk_of[:, None]
        gate = jnp.where(past[None, :, None, :], gate, -jnp.inf)
        _, sel_idx = lax.top_k(gate, n_top)
        sel_valid = sel_idx < blk_of[None, :, None, None]

    def chunk(c):
        t0 = c * Q_CHUNK_A
        qpos = t0 + jnp.arange(Q_CHUNK_A)
        q_c = lax.dynamic_slice_in_dim(q, t0, Q_CHUNK_A, axis=1)
        s0 = (t0 // BLOCK_A) * BLOCK_A
        k_own = lax.dynamic_slice_in_dim(kp, s0, BLOCK_A, axis=1)
        v_own = lax.dynamic_slice_in_dim(vp, s0, BLOCK_A, axis=1)
        own_pos = s0 + jnp.arange(BLOCK_A)
        if n_top == 0:
            return _moba_core(q_c, qpos, None, None, None, None, k_own, v_own, own_pos, slopes)
        idx = lax.dynamic_slice_in_dim(sel_idx, t0, Q_CHUNK_A, axis=1)
        val = lax.dynamic_slice_in_dim(sel_valid, t0, Q_CHUNK_A, axis=1)
        k_sel = jax.vmap(lambda kb, ix: kb[ix])(k_blk, idx)
        v_sel = jax.vmap(lambda vb, ix: vb[ix])(v_blk, idx)
        sel_pos = idx[..., None] * BLOCK_A + jnp.arange(BLOCK_A)
        return _moba_core(q_c, qpos, k_sel, v_sel, sel_pos, val, k_own, v_own, own_pos, slopes)

    out = lax.map(chunk, jnp.arange(s // Q_CHUNK_A))
    return jnp.moveaxis(out, 0, 1).reshape(b, s, h, d)


def _moba_sample(q, k_new, v_new, cache_k, cache_v, layer, page_table, slopes):
    db, nq, h, d = q.shape
    n_pages = page_table.shape[1]
    past = n_pages * PAGE_SIZE
    ppb = BLOCK_A // PAGE_SIZE
    b_own = past // BLOCK_A
    own_start = b_own * BLOCK_A
    qpos = past + jnp.arange(nq)
    k_past = cache_k[layer, page_table].reshape(db, past, d)
    k_own = jnp.concatenate([k_past[:, own_start:], k_new], axis=1)
    if past > own_start:
        v_prev = cache_v[layer, page_table[:, own_start // PAGE_SIZE:]].reshape(db, past - own_start, d)
        v_own = jnp.concatenate([v_prev, v_new], axis=1)
    else:
        v_own = v_new
    own_pos = own_start + jnp.arange(past - own_start + nq)
    n_top = min(TOP_K_A, b_own)
    if n_top == 0:
        return _moba_core(q, qpos, None, None, None, None, k_own, v_own, own_pos, slopes)
    k_mean = jnp.mean(k_past[:, :own_start].reshape(db, b_own, BLOCK_A, d).astype(jnp.float32), axis=2)
    gate = jnp.einsum('bqhd,bnd->bqhn', q.astype(jnp.float32), k_mean)
    _, sel_idx = lax.top_k(gate, n_top)
    sel_valid = jnp.ones(sel_idx.shape, dtype=bool)
    logical = sel_idx[..., None] * ppb + jnp.arange(ppb)
    phys = jax.vmap(lambda pt, lg: pt[lg])(page_table, logical)
    k_sel = cache_k[layer, phys].reshape(db, nq, h, n_top, BLOCK_A, d)
    v_sel = cache_v[layer, phys].reshape(db, nq, h, n_top, BLOCK_A, d)
    sel_pos = sel_idx[..., None] * BLOCK_A + jnp.arange(BLOCK_A)
    return _moba_core(q, qpos, k_sel, v_sel, sel_pos, sel_valid, k_own, v_own, own_pos, slopes)


def _mla_attend(q_lat, q_rope, segs, qpos):
    scale = (NOPE_DIM + ROPE_DIM) ** -0.5
    scores = []
    for ckv, kr, kpos in segs:
        sc = (jnp.einsum('bqhc,blc->bhql', q_lat, ckv) + jnp.einsum('bqhr,blr->bhql', q_rope, kr)).astype(jnp.float32) * scale
        sc = jnp.where((kpos[None, :] <= qpos[:, None])[None, None], sc, -jnp.inf)
        scores.append(sc)
    p = jax.nn.softmax(jnp.concatenate(scores, axis=-1), axis=-1)
    out = None
    off = 0
    for ckv, _, kpos in segs:
        n = kpos.shape[0]
        term = jnp.einsum('bhql,blc->bqhc', p[..., off:off + n].astype(ckv.dtype), ckv)
        out = term if out is None else out + term
        off += n
    return out


def _mla_prompt(q_lat, q_rope, ckv, krope):
    b, s, h, c = q_lat.shape
    kpos = jnp.arange(s)

    def blk(i):
        t0 = i * ATTN_Q_BLOCK
        ql = lax.dynamic_slice_in_dim(q_lat, t0, ATTN_Q_BLOCK, axis=1)
        qr = lax.dynamic_slice_in_dim(q_rope, t0, ATTN_Q_BLOCK, axis=1)
        return _mla_attend(ql, qr, [(ckv, krope, kpos)], t0 + jnp.arange(ATTN_Q_BLOCK))

    out = lax.map(blk, jnp.arange(s // ATTN_Q_BLOCK))
    return jnp.moveaxis(out, 0, 1).reshape(b, s, h, c)


def _mla_sample(q_lat, q_rope, ckv_new, kr_new, cache_ckv, cache_kr, layer, page_table):
    db, nq = q_lat.shape[:2]
    past = page_table.shape[1] * PAGE_SIZE
    ckv_past = cache_ckv[layer, page_table].reshape(db, past, KV_LORA)
    kr_past = cache_kr[layer, page_table].reshape(db, past, ROPE_DIM)
    qpos = past + jnp.arange(nq)
    segs = [(ckv_past, kr_past, jnp.arange(past)), (ckv_new, kr_new, qpos)]
    return _mla_attend(q_lat, q_rope, segs, qpos)


def _finish(x, att_a, att_lat, gate_a, gate_b, w_uv, w_oa, w_ob, w_out, ln1_g, ln1_b, w_up, w_down, ln2_g, ln2_b):
    b, s, _ = x.shape
    o_a = att_a.reshape(b, s, N_HEADS_A * HEAD_DIM_A) @ w_oa
    o_b = jnp.einsum('bshc,chv->bshv', att_lat, w_uv).reshape(b, s, N_HEADS_B * V_DIM_B) @ w_ob
    mix = (gate_a * o_a + gate_b * o_b) @ w_out
    x = _layernorm(ALPHA * x + mix, ln1_g, ln1_b)
    ff = jnp.square(jax.nn.relu(x @ w_up)) @ w_down
    return _layernorm(ALPHA * x + ff, ln2_g, ln2_b)


def setup_inputs(seed: int = 0) -> dict:
    key = jax.random.key(seed)
    ks = jax.random.split(key, 24)
    n_pages = PAST_LEN // PAGE_SIZE
    n_used = DEC_BATCH * n_pages
    n_pool = n_used + n_used // 4
    f32 = jnp.float32

    def nrm(k, shape, s=1.0):
        return jax.random.normal(k, shape, f32) * s

    page_table = jax.random.permutation(ks[6], n_pool)[:n_used].reshape(DEC_BATCH, n_pages).astype(jnp.int32)
    return {
        'x_prompt': nrm(ks[0], (BATCH, SEQ, D_MODEL)),
        'x_sample': nrm(ks[1], (DEC_BATCH, DEC_SEQ, D_MODEL)),
        'cache_moba_k': nrm(ks[2], (DEPTH, n_pool, PAGE_SIZE, HEAD_DIM_A)),
        'cache_moba_v': nrm(ks[3], (DEPTH, n_pool, PAGE_SIZE, HEAD_DIM_A)),
        'cache_mla_ckv': nrm(ks[4], (DEPTH, n_pool, PAGE_SIZE, KV_LORA)),
        'cache_mla_krope': nrm(ks[5], (DEPTH, n_pool, PAGE_SIZE, ROPE_DIM)),
        'page_table': page_table,
        'w_in': nrm(ks[7], (DEPTH, D_MODEL, N_IN), D_MODEL ** -0.5),
        'w_uq': nrm(ks[8], (DEPTH, Q_LORA, N_HEADS_B * (NOPE_DIM + ROPE_DIM)), Q_LORA ** -0.5),
        'g_qn': 1.0 + nrm(ks[9], (DEPTH, Q_LORA), 0.01),
        'g_kvn': 1.0 + nrm(ks[10], (DEPTH, KV_LORA), 0.01),
        'w_uk': nrm(ks[11], (DEPTH, KV_LORA, N_HEADS_B, NOPE_DIM), KV_LORA ** -0.5),
        'w_uv': nrm(ks[12], (DEPTH, KV_LORA, N_HEADS_B, V_DIM_B), KV_LORA ** -0.5),
        'w_oa': nrm(ks[13], (DEPTH, N_HEADS_A * HEAD_DIM_A, D_MODEL), (N_HEADS_A * HEAD_DIM_A) ** -0.5),
        'w_ob': nrm(ks[14], (DEPTH, N_HEADS_B * V_DIM_B, D_MODEL), (N_HEADS_B * V_DIM_B) ** -0.5),
        'w_out': nrm(ks[15], (DEPTH, D_MODEL, D_MODEL), BETA * D_MODEL ** -0.5),
        'ln1_g': 1.0 + nrm(ks[16], (DEPTH, D_MODEL), 0.01),
        'ln1_b': nrm(ks[17], (DEPTH, D_MODEL), 0.01),
        'w_up': nrm(ks[18], (DEPTH, D_MODEL, D_FF), D_MODEL ** -0.5),
        'w_down': nrm(ks[19], (DEPTH, D_FF, D_MODEL), BETA * D_FF ** -0.5),
        'ln2_g': 1.0 + nrm(ks[20], (DEPTH, D_MODEL), 0.01),
        'ln2_b': nrm(ks[21], (DEPTH, D_MODEL), 0.01),
    }


def reference(x_prompt, x_sample, cache_moba_k, cache_moba_v, cache_mla_ckv, cache_mla_krope, page_table,
              w_in, w_uq, g_qn, g_kvn, w_uk, w_uv, w_oa, w_ob, w_out, ln1_g, ln1_b, w_up, w_down, ln2_g, ln2_b):
    past = page_table.shape[1] * PAGE_SIZE
    dec_seq = x_sample.shape[1]
    assert BLOCK_A % PAGE_SIZE == 0 and (past % BLOCK_A) + dec_seq <= BLOCK_A
    slopes = _alibi_slopes()
    pos_p = jnp.arange(x_prompt.shape[1])
    pos_s = past + jnp.arange(dec_seq)
    xp, xs = x_prompt, x_sample
    pk, pv, pc, pr = [], [], [], []
    sk, sv, sc, sr = [], [], [], []
    for l in range(DEPTH):
        qa, ka, va, ql, qr, ckv, kr, ga, gb = _project(xp, pos_p, w_in[l], w_uq[l], g_qn[l], g_kvn[l], w_uk[l])
        att_a = _moba_prompt(qa, ka, va, slopes)
        att_b = _mla_prompt(ql, qr, ckv, kr)
        xp = _finish(xp, att_a, att_b, ga, gb, w_uv[l], w_oa[l], w_ob[l], w_out[l],
                     ln1_g[l], ln1_b[l], w_up[l], w_down[l], ln2_g[l], ln2_b[l])
        pk.append(ka); pv.append(va); pc.append(ckv); pr.append(kr)
        qa, ka, va, ql, qr, ckv, kr, ga, gb = _project(xs, pos_s, w_in[l], w_uq[l], g_qn[l], g_kvn[l], w_uk[l])
        att_a = _moba_sample(qa, ka, va, cache_moba_k, cache_moba_v, l, page_table, slopes)
        att_b = _mla_sample(ql, qr, ckv, kr, cache_mla_ckv, cache_mla_krope, l, page_table)
        xs = _finish(xs, att_a, att_b, ga, gb, w_uv[l], w_oa[l], w_ob[l], w_out[l],
                     ln1_g[l], ln1_b[l], w_up[l], w_down[l], ln2_g[l], ln2_b[l])
        sk.append(ka); sv.append(va); sc.append(ckv); sr.append(kr)
    return (xp, xs, jnp.stack(pk), jnp.stack(pv), jnp.stack(pc), jnp.stack(pr),
            jnp.stack(sk), jnp.stack(sv), jnp.stack(sc), jnp.stack(sr))
```

```python
import functools

import jax
import jax.numpy as jnp
from jax import lax
from jax.experimental import pallas as pl
from jax.experimental.pallas import tpu as pltpu

F32 = jnp.float32
BF16 = jnp.bfloat16

N_HEADS = 16
HEAD_DIM_A = 128
BLOCK_A = 256
TOP_K_A = 3
NOPE_DIM = 128
ROPE_DIM = 32
V_DIM_B = 128
Q_LORA = 1024
KV_LORA = 256
ROPE_THETA = 10000.0
PAGE_SIZE = 128
LN_EPS = 1e-5
RMS_EPS = 1e-6
LANES = 128
VMEM_LIMIT = 56 * 1024 * 1024
NEG_INF = float("-inf")

_NT = (((1,), (1,)), ((), ()))


def _pick(dim, target, align):
    best = None
    d = align
    while d <= min(dim, target):
        if dim % d == 0:
            best = d
        d += align
    return best if best is not None else dim


def _params(sem):
    return pltpu.CompilerParams(dimension_semantics=sem, vmem_limit_bytes=VMEM_LIMIT)


def _act(name, v):
    if name == "sigmoid":
        return jax.nn.sigmoid(v)
    if name == "relu2":
        r = jnp.maximum(v, 0.0)
        return r * r
    return v


def _mm_kernel(x_ref, w_ref, o_ref, *scratch, nk, act, nt):
    if nt:
        part = lax.dot_general(x_ref[...], w_ref[...], _NT, preferred_element_type=F32)
    else:
        part = jnp.dot(x_ref[...], w_ref[...], preferred_element_type=F32)
    if nk == 1:
        o_ref[...] = _act(act, part).astype(o_ref.dtype)
        return
    (acc_ref,) = scratch
    k = pl.program_id(2)

    @pl.when(k == 0)
    def _():
        acc_ref[...] = part

    @pl.when(k > 0)
    def _():
        acc_ref[...] += part

    @pl.when(k == nk - 1)
    def _():
        o_ref[...] = _act(act, acc_ref[...]).astype(o_ref.dtype)


def _matmul(x, w, *, out_dtype, act=None, nt=False, bm=1024, bn=1024, bk=4096, name="matmul"):
    m, kdim = x.shape
    n = w.shape[0] if nt else w.shape[1]
    bm = _pick(m, bm, 8)
    bn = _pick(n, bn, LANES)
    bk = _pick(kdim, bk, LANES)
    nk = kdim // bk
    w_spec = (pl.BlockSpec((bn, bk), lambda i, j, k: (j, k)) if nt
              else pl.BlockSpec((bk, bn), lambda i, j, k: (k, j)))
    return pl.pallas_call(
        functools.partial(_mm_kernel, nk=nk, act=act, nt=nt),
        grid=(m // bm, n // bn, nk),
        in_specs=[pl.BlockSpec((bm, bk), lambda i, j, k: (i, k)), w_spec],
        out_specs=pl.BlockSpec((bm, bn), lambda i, j, k: (i, j)),
        out_shape=jax.ShapeDtypeStruct((m, n), out_dtype),
        scratch_shapes=[pltpu.VMEM((bm, bn), F32)] if nk > 1 else [],
        compiler_params=_params(("parallel", "parallel", "arbitrary")),
        name=name,
    )(x, w)


def _gmm2_kernel(a_ref, wa_ref, b_ref, wb_ref, ga_ref, gb_ref, o_ref):
    oa = jnp.dot(a_ref[...], wa_ref[...], preferred_element_type=F32)
    ob = jnp.dot(b_ref[...], wb_ref[...], preferred_element_type=F32)
    o_ref[...] = (ga_ref[...].astype(F32) * oa + gb_ref[...].astype(F32) * ob).astype(o_ref.dtype)


def _gated_pair_matmul(a, wa, b, wb, gates):
    m, ka = a.shape
    kb = b.shape[1]
    n = wa.shape[1]
    bm = _pick(m, 1024, 8)
    bn = _pick(n, 1024, LANES)
    nj = n // bn
    return pl.pallas_call(
        _gmm2_kernel,
        grid=(m // bm, nj),
        in_specs=[
            pl.BlockSpec((bm, ka), lambda i, j: (i, 0)),
            pl.BlockSpec((ka, bn), lambda i, j: (0, j)),
            pl.BlockSpec((bm, kb), lambda i, j: (i, 0)),
            pl.BlockSpec((kb, bn), lambda i, j: (0, j)),
            pl.BlockSpec((bm, bn), lambda i, j: (i, j)),
            pl.BlockSpec((bm, bn), lambda i, j: (i, nj + j)),
        ],
        out_specs=pl.BlockSpec((bm, bn), lambda i, j: (i, j)),
        out_shape=jax.ShapeDtypeStruct((m, n), BF16),
        compiler_params=_params(("parallel", "parallel")),
        name="gated_pair_matmul",
    )(a, wa, b, wb, gates, gates)


def _headmm_kernel(x_ref, w_ref, o_ref):
    o_ref[...] = jnp.dot(x_ref[...].astype(BF16), w_ref[0], preferred_element_type=F32).astype(o_ref.dtype)


def _head_matmul(x, w, out_dtype):
    m = x.shape[0]
    nh, kh, nn = w.shape
    return pl.pallas_call(
        _headmm_kernel,
        grid=(nh,),
        in_specs=[pl.BlockSpec((m, kh), lambda h: (0, h)), pl.BlockSpec((1, kh, nn), lambda h: (h, 0, 0))],
        out_specs=pl.BlockSpec((m, nn), lambda h: (0, h)),
        out_shape=jax.ShapeDtypeStruct((m, nh * nn), out_dtype),
        compiler_params=_params(("parallel",)),
        name="head_matmul",
    )(x, w)


def _kvproj_kernel(x_ref, w_ref, g_ref, cos_ref, sin_ref, k_ref, v_ref, ckv_ref, kr_ref):
    h = jnp.dot(x_ref[...], w_ref[...], preferred_element_type=F32)
    d = HEAD_DIM_A
    k_ref[...] = h[:, 0:d]
    v_ref[...] = h[:, d:2 * d]
    c = h[:, 2 * d:2 * d + KV_LORA]
    ckv_ref[...] = c * lax.rsqrt(jnp.mean(c * c, axis=-1, keepdims=True) + RMS_EPS) * g_ref[...]
    o = 2 * d + KV_LORA
    r = h[:, o:o + ROPE_DIM]
    r_swapped = h[:, o + LANES:o + LANES + ROPE_DIM]
    kr_ref[...] = r * cos_ref[...] + r_swapped * sin_ref[...]


def _kv_project(xb, w_kv, g_kvn, cos_k, sin_k):
    m, dm = xb.shape
    bm = _pick(m, 1024, 8)
    nw = w_kv.shape[1]
    row = lambda i: (i, 0)
    return pl.pallas_call(
        _kvproj_kernel,
        grid=(m // bm,),
        in_specs=[
            pl.BlockSpec((bm, dm), row),
            pl.BlockSpec((dm, nw), lambda i: (0, 0)),
            pl.BlockSpec((1, KV_LORA), lambda i: (0, 0)),
            pl.BlockSpec((bm, ROPE_DIM), row),
            pl.BlockSpec((bm, ROPE_DIM), row),
        ],
        out_specs=[
            pl.BlockSpec((bm, HEAD_DIM_A), row),
            pl.BlockSpec((bm, HEAD_DIM_A), row),
            pl.BlockSpec((bm, KV_LORA), row),
            pl.BlockSpec((bm, ROPE_DIM), row),
        ],
        out_shape=[
            jax.ShapeDtypeStruct((m, HEAD_DIM_A), F32),
            jax.ShapeDtypeStruct((m, HEAD_DIM_A), F32),
            jax.ShapeDtypeStruct((m, KV_LORA), F32),
            jax.ShapeDtypeStruct((m, ROPE_DIM), F32),
        ],
        compiler_params=_params(("parallel",)),
        name="kv_project",
    )(xb, w_kv, g_kvn, cos_k, sin_k)


def _qproj_kernel(c_ref, g_ref, w_ref, cos_ref, sin_ref, qn_ref, qr_ref):
    c = c_ref[...].astype(F32)
    cn = (c * lax.rsqrt(jnp.mean(c * c, axis=-1, keepdims=True) + RMS_EPS) * g_ref[...]).astype(BF16)
    z = jnp.dot(cn, w_ref[...], preferred_element_type=F32)
    n_nope = N_HEADS * NOPE_DIM
    n_rope = N_HEADS * ROPE_DIM
    qn_ref[...] = z[:, :n_nope].astype(qn_ref.dtype)
    rope = z[:, n_nope:n_nope + n_rope] * cos_ref[...] + z[:, n_nope + n_rope:] * sin_ref[...]
    qr_ref[...] = rope.astype(qr_ref.dtype)


def _q_project(qc, g_qn, w_q, cos_q, sin_q):
    m = qc.shape[0]
    bm = _pick(m, 512, 8)
    nw = w_q.shape[1]
    n_nope = N_HEADS * NOPE_DIM
    n_rope = N_HEADS * ROPE_DIM
    c_blk = (N_HEADS * HEAD_DIM_A) // Q_LORA
    row = lambda i: (i, 0)
    return pl.pallas_call(
        _qproj_kernel,
        grid=(m // bm,),
        in_specs=[
            pl.BlockSpec((bm, Q_LORA), lambda i: (i, c_blk)),
            pl.BlockSpec((1, Q_LORA), lambda i: (0, 0)),
            pl.BlockSpec((Q_LORA, nw), lambda i: (0, 0)),
            pl.BlockSpec((bm, n_rope), row),
            pl.BlockSpec((bm, n_rope), row),
        ],
        out_specs=[pl.BlockSpec((bm, n_nope), row), pl.BlockSpec((bm, n_rope), row)],
        out_shape=[jax.ShapeDtypeStruct((m, n_nope), BF16), jax.ShapeDtypeStruct((m, n_rope), BF16)],
        compiler_params=_params(("parallel",)),
        name="q_project",
    )(qc, g_qn, w_q, cos_q, sin_q)


def _ln_kernel(x_ref, y_ref, g_ref, b_ref, o_ref, ob_ref, *, alpha):
    z = alpha * x_ref[...] + y_ref[...]
    mu = jnp.mean(z, axis=-1, keepdims=True)
    zc = z - mu
    var = jnp.mean(zc * zc, axis=-1, keepdims=True)
    o = zc * lax.rsqrt(var + LN_EPS) * g_ref[...] + b_ref[...]
    o_ref[...] = o
    ob_ref[...] = o.astype(ob_ref.dtype)


def _residual_layernorm(x, y, g, b, alpha):
    m, dm = x.shape
    bm = _pick(m, 256, 8)
    row = lambda i: (i, 0)
    vec = lambda i: (0, 0)
    return pl.pallas_call(
        functools.partial(_ln_kernel, alpha=alpha),
        grid=(m // bm,),
        in_specs=[pl.BlockSpec((bm, dm), row), pl.BlockSpec((bm, dm), row),
                  pl.BlockSpec((1, dm), vec), pl.BlockSpec((1, dm), vec)],
        out_specs=[pl.BlockSpec((bm, dm), row), pl.BlockSpec((bm, dm), row)],
        out_shape=[jax.ShapeDtypeStruct((m, dm), F32), jax.ShapeDtypeStruct((m, dm), BF16)],
        compiler_params=_params(("parallel",)),
        name="residual_layernorm",
    )(x, y, g, b)


def _split3(v):
    hi = v.astype(BF16)
    r1 = v - hi.astype(F32)
    mid = r1.astype(BF16)
    lo = (r1 - mid.astype(F32)).astype(BF16)
    return hi, mid, lo


def _online_step(s, m, l, acc, vt):
    m_new = jnp.maximum(m, jnp.max(s, axis=0, keepdims=True))
    alpha = jnp.exp(m - m_new)
    p = jnp.exp(s - m_new)
    l_new = alpha * l + jnp.sum(p, axis=0, keepdims=True)
    acc_new = alpha * acc + jnp.dot(vt, p.astype(BF16), preferred_element_type=F32)
    return m_new, l_new, acc_new


def _moba_prompt_kernel(q_ref, k_ref, vt_ref, o_ref, kb_ref, kmean_ref, sel_ref, *, nb, scale):
    i = pl.program_id(1)
    blk = BLOCK_A
    d = HEAD_DIM_A

    @pl.when(i == 0)
    def _per_batch():
        for n in range(nb):
            kn = k_ref[0, n * blk:(n + 1) * blk, :]
            kmean_ref[n:n + 1, :] = jnp.mean(kn, axis=0, keepdims=True)
            kb_ref[n * blk:(n + 1) * blk, :] = kn.astype(BF16)

    key_j = lax.broadcasted_iota(jnp.int32, (blk, blk), 0)
    tok_t = lax.broadcasted_iota(jnp.int32, (blk, blk), 1)
    dist0 = (tok_t - key_j).astype(F32)
    causal = tok_t >= key_j
    blk_idx = lax.broadcasted_iota(jnp.int32, (nb, blk), 0)
    past = blk_idx < i
    km_hi, km_mid, km_lo = _split3(kmean_ref[...])
    own0 = pl.multiple_of(i * blk, blk)
    k_own = kb_ref[pl.ds(own0, blk), :]
    vt_own = vt_ref[0, i]

    for h in range(N_HEADS):
        slope = 2.0 ** (-8.0 * (h + 1) / N_HEADS)
        q_h = q_ref[:, h * d:(h + 1) * d]
        gate = (lax.dot_general(km_hi, q_h, _NT, preferred_element_type=F32)
                + lax.dot_general(km_mid, q_h, _NT, preferred_element_type=F32)
                + lax.dot_general(km_lo, q_h, _NT, preferred_element_type=F32))
        gate = jnp.where(past, gate, NEG_INF)
        rank = jnp.zeros((nb, blk), F32)
        for mth in range(nb):
            gm = gate[mth:mth + 1, :]
            beats = (gm > gate) | ((gm == gate) & (blk_idx > mth))
            rank = rank + jnp.where(beats, 1.0, 0.0)
        sel_ref[...] = jnp.where((rank < TOP_K_A) & past, 1.0, 0.0)
        bias0 = slope * dist0

        s = lax.dot_general(k_own, q_h, _NT, preferred_element_type=F32) * scale - bias0
        s = jnp.where(causal, s, NEG_INF)
        m0 = jnp.max(s, axis=0, keepdims=True)
        p = jnp.exp(s - m0)
        l0 = jnp.sum(p, axis=0, keepdims=True)
        acc0 = jnp.dot(vt_own, p.astype(BF16), preferred_element_type=F32)

        def body(n, carry, q_h=q_h, bias0=bias0, slope=slope):
            m, l, acc = carry
            k0 = pl.multiple_of(n * blk, blk)
            kn = kb_ref[pl.ds(k0, blk), :]
            shift = slope * blk * lax.convert_element_type(i - n, F32)
            s = lax.dot_general(kn, q_h, _NT, preferred_element_type=F32) * scale - (bias0 + shift)
            s = jnp.where(sel_ref[pl.ds(n, 1), :] > 0.0, s, NEG_INF)
            return _online_step(s, m, l, acc, vt_ref[0, n])

        m, l, acc = lax.fori_loop(0, i, body, (m0, l0, acc0))
        o_ref[:, h * d:(h + 1) * d] = (acc / l).T.astype(o_ref.dtype)


def _moba_prompt(q2d, q_col_blocks, k, vt, batch, seq):
    assert seq % BLOCK_A == 0
    nb = seq // BLOCK_A
    hd = N_HEADS * HEAD_DIM_A
    del q_col_blocks
    return pl.pallas_call(
        functools.partial(_moba_prompt_kernel, nb=nb, scale=HEAD_DIM_A ** -0.5),
        grid=(batch, nb),
        in_specs=[
            pl.BlockSpec((BLOCK_A, hd), lambda b, i: (b * nb + i, 0)),
            pl.BlockSpec((1, seq, HEAD_DIM_A), lambda b, i: (b, 0, 0)),
            pl.BlockSpec((1, nb, HEAD_DIM_A, BLOCK_A), lambda b, i: (b, 0, 0, 0)),
        ],
        out_specs=pl.BlockSpec((BLOCK_A, hd), lambda b, i: (b * nb + i, 0)),
        out_shape=jax.ShapeDtypeStruct((batch * seq, hd), BF16),
        scratch_shapes=[
            pltpu.VMEM((seq, HEAD_DIM_A), BF16),
            pltpu.VMEM((nb, HEAD_DIM_A), F32),
            pltpu.VMEM((nb, BLOCK_A), F32),
        ],
        compiler_params=_params(("parallel", "arbitrary")),
        name="moba_prompt",
    )(q2d, k, vt)


MLA_TILE = 256


def _mla_prompt_kernel(qn_ref, qr_ref, kn_ref, kr_ref, vt_ref, o_ref, *, scale):
    i = pl.program_id(2)
    t = MLA_TILE
    q = jnp.concatenate([qn_ref[...], qr_ref[...]], axis=1)

    def keys(k0):
        return jnp.concatenate([kn_ref[pl.ds(k0, t), :], kr_ref[0, 0, pl.ds(k0, t), :]], axis=1)

    key_j = lax.broadcasted_iota(jnp.int32, (t, t), 0)
    tok_t = lax.broadcasted_iota(jnp.int32, (t, t), 1)
    own0 = pl.multiple_of(i * t, t)
    s = lax.dot_general(keys(own0), q, _NT, preferred_element_type=F32) * scale
    s = jnp.where(tok_t >= key_j, s, NEG_INF)
    m0 = jnp.max(s, axis=0, keepdims=True)
    p = jnp.exp(s - m0)
    l0 = jnp.sum(p, axis=0, keepdims=True)
    acc0 = jnp.dot(vt_ref[0, 0, i], p.astype(BF16), preferred_element_type=F32)

    def body(n, carry):
        m, l, acc = carry
        k0 = pl.multiple_of(n * t, t)
        s = lax.dot_general(keys(k0), q, _NT, preferred_element_type=F32) * scale
        return _online_step(s, m, l, acc, vt_ref[0, 0, n])

    m, l, acc = lax.fori_loop(0, i, body, (m0, l0, acc0))
    o_ref[...] = (acc / l).T.astype(o_ref.dtype)


def _mla_prompt(qn, qr, kn, kr_exp, vt, batch, seq):
    t = MLA_TILE
    assert seq % t == 0
    nq = seq // t
    heads_per_lane_group = LANES // ROPE_DIM
    return pl.pallas_call(
        functools.partial(_mla_prompt_kernel, scale=(NOPE_DIM + ROPE_DIM) ** -0.5),
        grid=(batch, N_HEADS, nq),
        in_specs=[
            pl.BlockSpec((t, NOPE_DIM), lambda b, h, i: (b * nq + i, h)),
            pl.BlockSpec((t, LANES), lambda b, h, i: (b * nq + i, h // heads_per_lane_group)),
            pl.BlockSpec((seq, NOPE_DIM), lambda b, h, i: (b, h)),
            pl.BlockSpec((1, 1, seq, LANES), lambda b, h, i: (b, h % heads_per_lane_group, 0, 0)),
            pl.BlockSpec((1, 1, nq, V_DIM_B, t), lambda b, h, i: (b, h, 0, 0, 0)),
        ],
        out_specs=pl.BlockSpec((t, V_DIM_B), lambda b, h, i: (b * nq + i, h)),
        out_shape=jax.ShapeDtypeStruct((batch * seq, N_HEADS * V_DIM_B), BF16),
        compiler_params=_params(("parallel", "parallel", "arbitrary")),
        name="mla_prompt",
    )(qn, qr, kn, kr_exp, vt)


CHUNK_PAGES = 16


def _page_copies(pt_ref, b, first_page, hbm, layer, buf, slot, sem, n):
    return [
        pltpu.make_async_copy(
            hbm.at[layer, pt_ref[b, first_page + p]],
            buf.at[slot, pl.ds(p * PAGE_SIZE, PAGE_SIZE)],
            sem.at[slot],
        )
        for p in range(n)
    ]


def _mla_decode_kernel(pt_ref, ql_ref, qr_ref, cnew_ref, rnew_ref, ckv_hbm, kr_hbm, o_ref,
                       cbuf, rbuf, csem, rsem, *, layer, n_pages, scale):
    b = pl.program_id(0)
    cp = CHUNK_PAGES
    nchunks = n_pages // cp

    def copies(c, slot):
        return (_page_copies(pt_ref, b, c * cp, ckv_hbm, layer, cbuf, slot, csem, cp)
                + _page_copies(pt_ref, b, c * cp, kr_hbm, layer, rbuf, slot, rsem, cp))

    for cpy in copies(0, 0):
        cpy.start()
    ql = ql_ref[0]
    qr = qr_ref[0]

    def body(c, carry):
        m, l, acc = carry
        slot = c % 2
        for cpy in copies(c, slot):
            cpy.wait()

        @pl.when(c + 1 < nchunks)
        def _():
            for cpy in copies(c + 1, 1 - slot):
                cpy.start()

        kc = cbuf[slot].astype(BF16)
        kr = rbuf[slot].astype(BF16)
        s = (lax.dot_general(ql, kc, _NT, preferred_element_type=F32)
             + lax.dot_general(qr, kr, _NT, preferred_element_type=F32)) * scale
        m_new = jnp.maximum(m, jnp.max(s, axis=1, keepdims=True))
        alpha = jnp.exp(m - m_new)
        p = jnp.exp(s - m_new)
        l_new = alpha * l + jnp.sum(p, axis=1, keepdims=True)
        acc_new = alpha * acc + jnp.dot(p.astype(BF16), kc, preferred_element_type=F32)
        return m_new, l_new, acc_new

    init = (jnp.full((N_HEADS, 1), NEG_INF, F32), jnp.zeros((N_HEADS, 1), F32),
            jnp.zeros((N_HEADS, KV_LORA), F32))
    m, l, acc = lax.fori_loop(0, nchunks, body, init)
    cnew = cnew_ref[0]
    rnew = rnew_ref[0]
    s_new = (jnp.sum(ql.astype(F32) * cnew, axis=1, keepdims=True)
             + jnp.sum(qr.astype(F32) * rnew, axis=1, keepdims=True)) * scale
    m_f = jnp.maximum(m, s_new)
    a = jnp.exp(m - m_f)
    pn = jnp.exp(s_new - m_f)
    o_ref[0] = (a * acc + pn * cnew) / (a * l + pn)


def _mla_decode(page_table, q_lat, q_rope, ckv_new, kr_new, cache_ckv, cache_kr, layer):
    db, n_pages = page_table.shape
    assert n_pages % CHUNK_PAGES == 0
    rows = CHUNK_PAGES * PAGE_SIZE
    seq3 = lambda b, pt: (b, 0, 0)
    grid_spec = pltpu.PrefetchScalarGridSpec(
        num_scalar_prefetch=1,
        grid=(db,),
        in_specs=[
            pl.BlockSpec((1, N_HEADS, KV_LORA), seq3),
            pl.BlockSpec((1, N_HEADS, ROPE_DIM), seq3),
            pl.BlockSpec((1, 1, KV_LORA), seq3),
            pl.BlockSpec((1, 1, ROPE_DIM), seq3),
            pl.BlockSpec(memory_space=pl.ANY),
            pl.BlockSpec(memory_space=pl.ANY),
        ],
        out_specs=pl.BlockSpec((1, N_HEADS, KV_LORA), seq3),
        scratch_shapes=[
            pltpu.VMEM((2, rows, KV_LORA), F32),
            pltpu.VMEM((2, rows, ROPE_DIM), F32),
            pltpu.SemaphoreType.DMA((2,)),
            pltpu.SemaphoreType.DMA((2,)),
        ],
    )
    return pl.pallas_call(
        functools.partial(_mla_decode_kernel, layer=layer, n_pages=n_pages,
                          scale=(NOPE_DIM + ROPE_DIM) ** -0.5),
        grid_spec=grid_spec,
        out_shape=jax.ShapeDtypeStruct((db, N_HEADS, KV_LORA), F32),
        compiler_params=_params(("arbitrary",)),
        name="mla_decode",
    )(page_table, q_lat, q_rope, ckv_new, kr_new, cache_ckv, cache_kr)


def _moba_decode_kernel(pt_ref, q_ref, knew_ref, vnew_ref, k_hbm, v_hbm, o_ref,
                        kbuf, vbuf, ksem, vsem, kmean_ref, m_ref, l_ref, acc_ref,
                        *, layer, n_pages, scale):
    b = pl.program_id(0)
    cp = CHUNK_PAGES
    nchunks = n_pages // cp
    blk = BLOCK_A
    d = HEAD_DIM_A
    blocks_per_chunk = cp * PAGE_SIZE // blk
    nblk = n_pages * PAGE_SIZE // blk
    nblk_pad = kmean_ref.shape[0]
    past = n_pages * PAGE_SIZE

    def copies(c, slot):
        return (_page_copies(pt_ref, b, c * cp, k_hbm, layer, kbuf, slot, ksem, cp)
                + _page_copies(pt_ref, b, c * cp, v_hbm, layer, vbuf, slot, vsem, cp))

    for cpy in copies(0, 0):
        cpy.start()
    q = q_ref[0]
    head = lax.broadcasted_iota(jnp.int32, (N_HEADS, 1), 0).astype(F32)
    slopes = jnp.exp2(-8.0 * (head + 1.0) / N_HEADS)
    key_off = lax.broadcasted_iota(jnp.int32, (1, blk), 1).astype(F32)
    if nblk_pad > nblk:
        kmean_ref[nblk:, :] = jnp.zeros((nblk_pad - nblk, d), F32)

    def chunk(c, carry):
        slot = c % 2
        for cpy in copies(c, slot):
            cpy.wait()

        @pl.when(c + 1 < nchunks)
        def _():
            for cpy in copies(c + 1, 1 - slot):
                cpy.start()

        for j in range(blocks_per_chunk):
            n = c * blocks_per_chunk + j
            kn = kbuf[slot, j * blk:(j + 1) * blk, :]
            kmean_ref[pl.ds(n, 1), :] = jnp.mean(kn, axis=0, keepdims=True)
            dist = lax.convert_element_type(past - n * blk, F32) - key_off
            s = lax.dot_general(q, kn.astype(BF16), _NT, preferred_element_type=F32) * scale - slopes * dist
            mb = jnp.max(s, axis=1, keepdims=True)
            p = jnp.exp(s - mb)
            lb = jnp.sum(p, axis=1, keepdims=True)
            vn = vbuf[slot, j * blk:(j + 1) * blk, :].astype(BF16)
            acc_ref[n] = jnp.dot(p.astype(BF16), vn, preferred_element_type=F32)
            m_ref[n] = jnp.broadcast_to(mb, (N_HEADS, d))
            l_ref[n] = jnp.broadcast_to(lb, (N_HEADS, d))
        return carry

    lax.fori_loop(0, nchunks, chunk, 0)

    km_hi, km_mid, km_lo = _split3(kmean_ref[...])
    gate = (lax.dot_general(q, km_hi, _NT, preferred_element_type=F32)
            + lax.dot_general(q, km_mid, _NT, preferred_element_type=F32)
            + lax.dot_general(q, km_lo, _NT, preferred_element_type=F32))
    idx = lax.broadcasted_iota(jnp.int32, (N_HEADS, nblk_pad), 1).astype(F32)
    gate = jnp.where(idx < nblk, gate, NEG_INF)
    sel = jnp.zeros((N_HEADS, nblk_pad), F32)
    for _ in range(min(TOP_K_A, nblk)):
        mx = jnp.max(gate, axis=1, keepdims=True)
        first = jnp.min(jnp.where(gate == mx, idx, float(nblk_pad)), axis=1, keepdims=True)
        pick = idx == first
        sel = jnp.where(pick, 1.0, sel)
        gate = jnp.where(pick, NEG_INF, gate)

    knew = knew_ref[0]
    vnew = vnew_ref[0]
    s_own = jnp.sum(q.astype(F32) * knew, axis=1, keepdims=True) * scale
    m_all = jnp.broadcast_to(s_own, (N_HEADS, d))
    sel_b = []
    for n in range(nblk):
        sn = jnp.broadcast_to(sel[:, n:n + 1], (N_HEADS, d)) > 0.0
        sel_b.append(sn)
        m_all = jnp.maximum(m_all, jnp.where(sn, m_ref[n], NEG_INF))
    w_own = jnp.exp(jnp.broadcast_to(s_own, (N_HEADS, d)) - m_all)
    l_all = w_own
    acc_all = w_own * vnew
    for n in range(nblk):
        w = jnp.where(sel_b[n], jnp.exp(m_ref[n] - m_all), 0.0)
        l_all = l_all + w * l_ref[n]
        acc_all = acc_all + w * acc_ref[n]
    o_ref[0] = acc_all / l_all


def _moba_decode(page_table, q, k_new, v_new, cache_k, cache_v, layer):
    db, n_pages = page_table.shape
    assert n_pages % CHUNK_PAGES == 0 and (n_pages * PAGE_SIZE) % BLOCK_A == 0
    rows = CHUNK_PAGES * PAGE_SIZE
    nblk = n_pages * PAGE_SIZE // BLOCK_A
    nblk_pad = -(-nblk // LANES) * LANES
    d = HEAD_DIM_A
    seq3 = lambda b, pt: (b, 0, 0)
    grid_spec = pltpu.PrefetchScalarGridSpec(
        num_scalar_prefetch=1,
        grid=(db,),
        in_specs=[
            pl.BlockSpec((1, N_HEADS, d), seq3),
            pl.BlockSpec((1, 1, d), seq3),
            pl.BlockSpec((1, 1, d), seq3),
            pl.BlockSpec(memory_space=pl.ANY),
            pl.BlockSpec(memory_space=pl.ANY),
        ],
        out_specs=pl.BlockSpec((1, N_HEADS, d), seq3),
        scratch_shapes=[
            pltpu.VMEM((2, rows, d), F32),
            pltpu.VMEM((2, rows, d), F32),
            pltpu.SemaphoreType.DMA((2,)),
            pltpu.SemaphoreType.DMA((2,)),
            pltpu.VMEM((nblk_pad, d), F32),
            pltpu.VMEM((nblk, N_HEADS, d), F32),
            pltpu.VMEM((nblk, N_HEADS, d), F32),
            pltpu.VMEM((nblk, N_HEADS, d), F32),
        ],
    )
    return pl.pallas_call(
        functools.partial(_moba_decode_kernel, layer=layer, n_pages=n_pages, scale=HEAD_DIM_A ** -0.5),
        grid_spec=grid_spec,
        out_shape=jax.ShapeDtypeStruct((db, N_HEADS, d), F32),
        compiler_params=_params(("arbitrary",)),
        name="moba_decode",
    )(page_table, q, k_new, v_new, cache_k, cache_v)


def _rope_tables(pos):
    half = ROPE_DIM // 2
    inv = jnp.power(ROPE_THETA, -2.0 * jnp.arange(half, dtype=F32) / ROPE_DIM)
    ang = pos.astype(F32)[:, None] * inv[None, :]
    cos, sin = jnp.cos(ang), jnp.sin(ang)
    return jnp.concatenate([cos, cos], -1), jnp.concatenate([-sin, sin], -1)


def _layer_weights(l, w_in, w_uq, g_qn, g_kvn, w_uk, w_uv, w_oa, w_ob, w_out, ln1_g, ln1_b,
                   w_up, w_down, ln2_g, ln2_b):
    dm = w_in.shape[1]
    win = w_in[l]
    hd = N_HEADS * HEAD_DIM_A
    o_k, o_v, o_cq = hd, hd + HEAD_DIM_A, hd + 2 * HEAD_DIM_A
    o_ckv = o_cq + Q_LORA
    o_kr = o_ckv + KV_LORA
    o_g = o_kr + ROPE_DIM
    half = ROPE_DIM // 2
    swap = jnp.concatenate([jnp.arange(half, ROPE_DIM), jnp.arange(half)])
    kr_cols = win[:, o_kr:o_g]
    pad = jnp.zeros((dm, LANES - ROPE_DIM), F32)
    w_kv = jnp.concatenate([win[:, o_k:o_cq], win[:, o_ckv:o_kr], kr_cols, pad, kr_cols[:, swap], pad], axis=1)
    wuq = w_uq[l].reshape(Q_LORA, N_HEADS, NOPE_DIM + ROPE_DIM)
    rope_cols = wuq[:, :, NOPE_DIM:]
    w_q = jnp.concatenate([
        wuq[:, :, :NOPE_DIM].reshape(Q_LORA, N_HEADS * NOPE_DIM),
        rope_cols.reshape(Q_LORA, N_HEADS * ROPE_DIM),
        rope_cols[:, :, swap].reshape(Q_LORA, N_HEADS * ROPE_DIM)], axis=1)
    return dict(
        w_qc=jnp.concatenate([win[:, :hd], win[:, o_cq:o_ckv]], axis=1).astype(BF16),
        w_kv=w_kv.astype(BF16),
        w_gate=win[:, o_g:].astype(BF16),
        w_q=w_q.astype(BF16),
        w_kup=w_uk[l].reshape(KV_LORA, N_HEADS * NOPE_DIM).astype(BF16),
        w_uv_t=w_uv[l].reshape(KV_LORA, N_HEADS * V_DIM_B).T.astype(BF16),
        w_uk_h=jnp.transpose(w_uk[l], (1, 2, 0)).astype(BF16),
        w_uv_h=jnp.transpose(w_uv[l], (1, 0, 2)).astype(BF16),
        w_oa=w_oa[l].astype(BF16), w_ob=w_ob[l].astype(BF16), w_out=w_out[l].astype(BF16),
        w_up=w_up[l].astype(BF16), w_down=w_down[l].astype(BF16),
        g_qn=g_qn[l][None, :], g_kvn=g_kvn[l][None, :],
        ln1_g=ln1_g[l][None, :], ln1_b=ln1_b[l][None, :], ln2_g=ln2_g[l][None, :], ln2_b=ln2_b[l][None, :],
    )


def _project(xb, w, cos_k, sin_k, cos_q, sin_q):
    qc = _matmul(xb, w["w_qc"], out_dtype=BF16, name="proj_q")
    k_a, v_a, ckv, krope = _kv_project(xb, w["w_kv"], w["g_kvn"], cos_k, sin_k)
    gates = _matmul(xb, w["w_gate"], out_dtype=BF16, act="sigmoid", name="proj_gates")
    qn, qr = _q_project(qc, w["g_qn"], w["w_q"], cos_q, sin_q)
    return qc, k_a, v_a, ckv, krope, gates, qn, qr


def _finish(x, xb_unused, att_a, att_b, gates, w, alpha):
    del xb_unused
    mixin = _gated_pair_matmul(att_a, w["w_oa"], att_b, w["w_ob"], gates)
    mix = _matmul(mixin, w["w_out"], out_dtype=F32, name="w_out")
    x1, x1b = _residual_layernorm(x, mix, w["ln1_g"], w["ln1_b"], alpha)
    hdn = _matmul(x1b, w["w_up"], out_dtype=BF16, act="relu2", name="ffn_up")
    ff = _matmul(hdn, w["w_down"], out_dtype=F32, bk=2048, name="ffn_down")
    return _residual_layernorm(x1, ff, w["ln2_g"], w["ln2_b"], alpha)


def kernel(x_prompt, x_sample, cache_moba_k, cache_moba_v, cache_mla_ckv, cache_mla_krope, page_table, w_in, w_uq, g_qn, g_kvn, w_uk, w_uv, w_oa, w_ob, w_out, ln1_g, ln1_b, w_up, w_down, ln2_g, ln2_b):
    depth, dm, _ = w_in.shape
    batch, seq, _ = x_prompt.shape
    db, dec_seq, _ = x_sample.shape
    n_pages = page_table.shape[1]
    past = n_pages * PAGE_SIZE
    assert dec_seq == 1 and past % BLOCK_A == 0 and cache_moba_k.shape[2] == PAGE_SIZE
    alpha = float((2 * depth) ** 0.25)
    hd = N_HEADS * HEAD_DIM_A
    p_rows = batch * seq

    cos_p, sin_p = _rope_tables(jnp.arange(seq))
    cos_s, sin_s = _rope_tables(jnp.full((db,), past))
    cos_kp, sin_kp = jnp.tile(cos_p, (batch, 1)), jnp.tile(sin_p, (batch, 1))
    cos_qp, sin_qp = jnp.tile(cos_kp, (1, N_HEADS)), jnp.tile(sin_kp, (1, N_HEADS))
    cos_qs, sin_qs = jnp.tile(cos_s, (1, N_HEADS)), jnp.tile(sin_s, (1, N_HEADS))
    lane_groups = LANES // ROPE_DIM

    xp = x_prompt.reshape(p_rows, dm)
    xs = x_sample.reshape(db, dm)
    xpb, xsb = xp.astype(BF16), xs.astype(BF16)
    new_p, new_s = [], []
    for l in range(depth):
        w = _layer_weights(l, w_in, w_uq, g_qn, g_kvn, w_uk, w_uv, w_oa, w_ob, w_out, ln1_g, ln1_b,
                           w_up, w_down, ln2_g, ln2_b)
        qc, k_a, v_a, ckv, krope, gates, qn, qr = _project(xpb, w, cos_kp, sin_kp, cos_qp, sin_qp)
        vt_a = jnp.swapaxes(v_a.reshape(batch, seq // BLOCK_A, BLOCK_A, HEAD_DIM_A), 2, 3).astype(BF16)
        att_a = _moba_prompt(qc, None, k_a.reshape(batch, seq, HEAD_DIM_A), vt_a, batch, seq)
        ckv_b = ckv.astype(BF16)
        k_nope = _matmul(ckv_b, w["w_kup"], out_dtype=BF16, name="k_up")
        vt_b = _matmul(w["w_uv_t"], ckv_b, out_dtype=BF16, nt=True, name="v_up_t")
        vt_b = jnp.transpose(vt_b.reshape(N_HEADS, V_DIM_B, batch, seq // MLA_TILE, MLA_TILE), (2, 0, 3, 1, 4))
        kr3 = krope.reshape(batch, seq, ROPE_DIM).astype(BF16)
        kr_exp = jnp.stack([jnp.pad(kr3, ((0, 0), (0, 0), (g * ROPE_DIM, LANES - (g + 1) * ROPE_DIM)))
                            for g in range(lane_groups)], axis=1)
        att_b = _mla_prompt(qn, qr, k_nope, kr_exp, vt_b, batch, seq)
        xp, xpb = _finish(xp, None, att_a, att_b, gates, w, alpha)
        new_p.append((k_a.reshape(batch, seq, -1), v_a.reshape(batch, seq, -1),
                      ckv.reshape(batch, seq, -1), krope.reshape(batch, seq, -1)))

        qc, k_a, v_a, ckv, krope, gates, qn, qr = _project(xsb, w, cos_s, sin_s, cos_qs, sin_qs)
        q_a = qc[:, :hd].reshape(db, N_HEADS, HEAD_DIM_A)
        att_a = _moba_decode(page_table, q_a, k_a.reshape(db, 1, -1), v_a.reshape(db, 1, -1),
                             cache_moba_k, cache_moba_v, l)
        q_lat = _head_matmul(qn, w["w_uk_h"], BF16).reshape(db, N_HEADS, KV_LORA)
        att_lat = _mla_decode(page_table, q_lat, qr.reshape(db, N_HEADS, ROPE_DIM),
                              ckv.reshape(db, 1, -1), krope.reshape(db, 1, -1),
                              cache_mla_ckv, cache_mla_krope, l)
        att_b = _head_matmul(att_lat.reshape(db, N_HEADS * KV_LORA), w["w_uv_h"], BF16)
        xs, xsb = _finish(xs, None, att_a.reshape(db, hd).astype(BF16), att_b, gates, w, alpha)
        new_s.append((k_a.reshape(db, 1, -1), v_a.reshape(db, 1, -1),
                      ckv.reshape(db, 1, -1), krope.reshape(db, 1, -1)))

    stack = lambda items, j: jnp.stack([it[j] for it in items])
    return (xp.reshape(batch, seq, dm), xs.reshape(db, dec_seq, dm),
            stack(new_p, 0), stack(new_p, 1), stack(new_p, 2), stack(new_p, 3),
            stack(new_s, 0), stack(new_s, 1), stack(new_s, 2), stack(new_s, 3))
```

```python
import functools

import jax
import jax.numpy as jnp
from jax import lax
from jax.experimental import pallas as pl
from jax.experimental.pallas import tpu as pltpu

F32 = jnp.float32
BF16 = jnp.bfloat16

N_HEADS = 16
HEAD_DIM_A = 128
BLOCK_A = 256
TOP_K_A = 3
NOPE_DIM = 128
ROPE_DIM = 32
V_DIM_B = 128
Q_LORA = 1024
KV_LORA = 256
ROPE_THETA = 10000.0
PAGE_SIZE = 128
LN_EPS = 1e-5
RMS_EPS = 1e-6
LANES = 128
VMEM_LIMIT = 56 * 1024 * 1024
NEG_INF = float("-inf")

_NT = (((1,), (1,)), ((), ()))


def _pick(dim, target, align):
    best = None
    d = align
    while d <= min(dim, target):
        if dim % d == 0:
            best = d
        d += align
    return best if best is not None else dim


def _params(sem):
    return pltpu.CompilerParams(dimension_semantics=sem, vmem_limit_bytes=VMEM_LIMIT)


def _act(name, v):
    if name == "sigmoid":
        return jax.nn.sigmoid(v)
    if name == "relu2":
        r = jnp.maximum(v, 0.0)
        return r * r
    return v


def _mm_kernel(x_ref, w_ref, o_ref, *scratch, nk, act, nt):
    if nt:
        part = lax.dot_general(x_ref[...], w_ref[...], _NT, preferred_element_type=F32)
    else:
        part = jnp.dot(x_ref[...], w_ref[...], preferred_element_type=F32)
    if nk == 1:
        o_ref[...] = _act(act, part).astype(o_ref.dtype)
        return
    (acc_ref,) = scratch
    k = pl.program_id(2)

    @pl.when(k == 0)
    def _():
        acc_ref[...] = part

    @pl.when(k > 0)
    def _():
        acc_ref[...] += part

    @pl.when(k == nk - 1)
    def _():
        o_ref[...] = _act(act, acc_ref[...]).astype(o_ref.dtype)


def _matmul(x, w, *, out_dtype, act=None, nt=False, bm=1024, bn=1024, bk=4096, name="matmul"):
    m, kdim = x.shape
    n = w.shape[0] if nt else w.shape[1]
    bm = _pick(m, bm, 8)
    bn = _pick(n, bn, LANES)
    bk = _pick(kdim, bk, LANES)
    nk = kdim // bk
    w_spec = (pl.BlockSpec((bn, bk), lambda i, j, k: (j, k)) if nt
              else pl.BlockSpec((bk, bn), lambda i, j, k: (k, j)))
    return pl.pallas_call(
        functools.partial(_mm_kernel, nk=nk, act=act, nt=nt),
        grid=(m // bm, n // bn, nk),
        in_specs=[pl.BlockSpec((bm, bk), lambda i, j, k: (i, k)), w_spec],
        out_specs=pl.BlockSpec((bm, bn), lambda i, j, k: (i, j)),
        out_shape=jax.ShapeDtypeStruct((m, n), out_dtype),
        scratch_shapes=[pltpu.VMEM((bm, bn), F32)] if nk > 1 else [],
        compiler_params=_params(("parallel", "parallel", "arbitrary")),
        name=name,
    )(x, w)


def _gmm2_kernel(a_ref, wa_ref, b_ref, wb_ref, ga_ref, gb_ref, o_ref):
    oa = jnp.dot(a_ref[...], wa_ref[...], preferred_element_type=F32)
    ob = jnp.dot(b_ref[...], wb_ref[...], preferred_element_type=F32)
    o_ref[...] = (ga_ref[...].astype(F32) * oa + gb_ref[...].astype(F32) * ob).astype(o_ref.dtype)


def _gated_pair_matmul(a, wa, b, wb, gates):
    m, ka = a.shape
    kb = b.shape[1]
    n = wa.shape[1]
    bm = _pick(m, 1024, 8)
    bn = _pick(n, 1024, LANES)
    nj = n // bn
    return pl.pallas_call(
        _gmm2_kernel,
        grid=(m // bm, nj),
        in_specs=[
            pl.BlockSpec((bm, ka), lambda i, j: (i, 0)),
            pl.BlockSpec((ka, bn), lambda i, j: (0, j)),
            pl.BlockSpec((bm, kb), lambda i, j: (i, 0)),
            pl.BlockSpec((kb, bn), lambda i, j: (0, j)),
            pl.BlockSpec((bm, bn), lambda i, j: (i, j)),
            pl.BlockSpec((bm, bn), lambda i, j: (i, nj + j)),
        ],
        out_specs=pl.BlockSpec((bm, bn), lambda i, j: (i, j)),
        out_shape=jax.ShapeDtypeStruct((m, n), BF16),
        compiler_params=_params(("parallel", "parallel")),
        name="gated_pair_matmul",
    )(a, wa, b, wb, gates, gates)


def _headmm_kernel(x_ref, w_ref, o_ref):
    o_ref[...] = jnp.dot(x_ref[...].astype(BF16), w_ref[0], preferred_element_type=F32).astype(o_ref.dtype)


def _head_matmul(x, w, out_dtype):
    m = x.shape[0]
    nh, kh, nn = w.shape
    return pl.pallas_call(
        _headmm_kernel,
        grid=(nh,),
        in_specs=[pl.BlockSpec((m, kh), lambda h: (0, h)), pl.BlockSpec((1, kh, nn), lambda h: (h, 0, 0))],
        out_specs=pl.BlockSpec((m, nn), lambda h: (0, h)),
        out_shape=jax.ShapeDtypeStruct((m, nh * nn), out_dtype),
        compiler_params=_params(("parallel",)),
        name="head_matmul",
    )(x, w)


def _kvproj_kernel(x_ref, w_ref, g_ref, cos_ref, sin_ref, k_ref, v_ref, ckv_ref, kr_ref):
    h = lax.dot_general(x_ref[...], w_ref[...], _NT, preferred_element_type=F32)
    d = HEAD_DIM_A
    k_ref[...] = h[:, 0:d]
    v_ref[...] = h[:, d:2 * d]
    c = h[:, 2 * d:2 * d + KV_LORA]
    ckv_ref[...] = c * lax.rsqrt(jnp.mean(c * c, axis=-1, keepdims=True) + RMS_EPS) * g_ref[...]
    o = 2 * d + KV_LORA
    r = h[:, o:o + ROPE_DIM]
    r_swapped = h[:, o + LANES:o + LANES + ROPE_DIM]
    kr_ref[...] = r * cos_ref[...] + r_swapped * sin_ref[...]


def _kv_project(xb, w_kv_t, g_kvn, cos_k, sin_k):
    m, dm = xb.shape
    bm = _pick(m, 1024, 8)
    nw = w_kv_t.shape[0]
    row = lambda i: (i, 0)
    return pl.pallas_call(
        _kvproj_kernel,
        grid=(m // bm,),
        in_specs=[
            pl.BlockSpec((bm, dm), row),
            pl.BlockSpec((nw, dm), lambda i: (0, 0)),
            pl.BlockSpec((1, KV_LORA), lambda i: (0, 0)),
            pl.BlockSpec((bm, ROPE_DIM), row),
            pl.BlockSpec((bm, ROPE_DIM), row),
        ],
        out_specs=[
            pl.BlockSpec((bm, HEAD_DIM_A), row),
            pl.BlockSpec((bm, HEAD_DIM_A), row),
            pl.BlockSpec((bm, KV_LORA), row),
            pl.BlockSpec((bm, ROPE_DIM), row),
        ],
        out_shape=[
            jax.ShapeDtypeStruct((m, HEAD_DIM_A), F32),
            jax.ShapeDtypeStruct((m, HEAD_DIM_A), F32),
            jax.ShapeDtypeStruct((m, KV_LORA), F32),
            jax.ShapeDtypeStruct((m, ROPE_DIM), F32),
        ],
        compiler_params=_params(("parallel",)),
        name="kv_project",
    )(xb, w_kv_t, g_kvn, cos_k, sin_k)


def _qproj_kernel(c_ref, g_ref, w_ref, cos_ref, sin_ref, qn_ref, qr_ref):
    c = c_ref[...].astype(F32)
    cn = (c * lax.rsqrt(jnp.mean(c * c, axis=-1, keepdims=True) + RMS_EPS) * g_ref[...]).astype(BF16)
    z = jnp.dot(cn, w_ref[...], preferred_element_type=F32)
    n_nope = N_HEADS * NOPE_DIM
    n_rope = N_HEADS * ROPE_DIM
    qn_ref[...] = z[:, :n_nope].astype(qn_ref.dtype)
    rope = z[:, n_nope:n_nope + n_rope] * cos_ref[...] + z[:, n_nope + n_rope:] * sin_ref[...]
    qr_ref[...] = rope.astype(qr_ref.dtype)


def _q_project(qc, g_qn, w_q, cos_q, sin_q):
    m = qc.shape[0]
    period = cos_q.shape[0]
    bm = _pick(period, 512, 8)
    nw = w_q.shape[1]
    n_nope = N_HEADS * NOPE_DIM
    n_rope = N_HEADS * ROPE_DIM
    c_blk = (N_HEADS * HEAD_DIM_A) // Q_LORA
    row = lambda i: (i, 0)
    tab = lambda i: (i % (period // bm), 0)
    return pl.pallas_call(
        _qproj_kernel,
        grid=(m // bm,),
        in_specs=[
            pl.BlockSpec((bm, Q_LORA), lambda i: (i, c_blk)),
            pl.BlockSpec((1, Q_LORA), lambda i: (0, 0)),
            pl.BlockSpec((Q_LORA, nw), lambda i: (0, 0)),
            pl.BlockSpec((bm, n_rope), tab),
            pl.BlockSpec((bm, n_rope), tab),
        ],
        out_specs=[pl.BlockSpec((bm, n_nope), row), pl.BlockSpec((bm, n_rope), row)],
        out_shape=[jax.ShapeDtypeStruct((m, n_nope), BF16), jax.ShapeDtypeStruct((m, n_rope), BF16)],
        compiler_params=_params(("parallel",)),
        name="q_project",
    )(qc, g_qn, w_q, cos_q, sin_q)


def _ln_kernel(x_ref, y_ref, g_ref, b_ref, o_ref, ob_ref, *, alpha):
    z = alpha * x_ref[...] + y_ref[...]
    mu = jnp.mean(z, axis=-1, keepdims=True)
    zc = z - mu
    var = jnp.mean(zc * zc, axis=-1, keepdims=True)
    o = zc * lax.rsqrt(var + LN_EPS) * g_ref[...] + b_ref[...]
    o_ref[...] = o
    ob_ref[...] = o.astype(ob_ref.dtype)


def _residual_layernorm(x, y, g, b, alpha):
    m, dm = x.shape
    bm = _pick(m, 256, 8)
    row = lambda i: (i, 0)
    vec = lambda i: (0, 0)
    return pl.pallas_call(
        functools.partial(_ln_kernel, alpha=alpha),
        grid=(m // bm,),
        in_specs=[pl.BlockSpec((bm, dm), row), pl.BlockSpec((bm, dm), row),
                  pl.BlockSpec((1, dm), vec), pl.BlockSpec((1, dm), vec)],
        out_specs=[pl.BlockSpec((bm, dm), row), pl.BlockSpec((bm, dm), row)],
        out_shape=[jax.ShapeDtypeStruct((m, dm), F32), jax.ShapeDtypeStruct((m, dm), BF16)],
        compiler_params=_params(("parallel",)),
        name="residual_layernorm",
    )(x, y, g, b)


Q_TILE = 256
EXTENT_STEP = 512
MOBA_HEADS_PER_STEP = 4


def _split3(v):
    hi = v.astype(BF16)
    r1 = v - hi.astype(F32)
    mid = r1.astype(BF16)
    lo = (r1 - mid.astype(F32)).astype(BF16)
    return hi, mid, lo


def _softmax_av_t(s, vt):
    m = jnp.max(s, axis=0, keepdims=True)
    p = jnp.exp(s - m)
    l = jnp.sum(p, axis=0, keepdims=True)
    out_t = jnp.dot(vt, p.astype(BF16), preferred_element_type=F32)
    return (out_t / l).T


def _extent_cases(i, seq):
    n_cases = -(-seq // EXTENT_STEP)
    return [(e, min((e + 1) * EXTENT_STEP, seq)) for e in range(n_cases)], (i * Q_TILE) // EXTENT_STEP


def _moba_prompt_kernel(q_ref, k_ref, vt_ref, o_ref, kb_ref, kmean_ref, *, nb, scale):
    i = pl.program_id(1)
    g = pl.program_id(2)
    blk = BLOCK_A
    d = HEAD_DIM_A
    seq = nb * blk

    @pl.when((i == 0) & (g == 0))
    def _per_batch():
        for n in range(nb):
            kn = k_ref[0, n * blk:(n + 1) * blk, :]
            kmean_ref[n:n + 1, :] = jnp.mean(kn, axis=0, keepdims=True)
            kb_ref[n * blk:(n + 1) * blk, :] = kn.astype(BF16)

    key_j = lax.broadcasted_iota(jnp.int32, (blk, blk), 0)
    tok_t = lax.broadcasted_iota(jnp.int32, (blk, blk), 1)
    dist0 = (tok_t - key_j).astype(F32)
    causal_pen = jnp.where(tok_t >= key_j, 0.0, NEG_INF)
    blk_idx = lax.broadcasted_iota(jnp.int32, (nb, blk), 0)
    past = blk_idx < i
    blocks_back = (i - blk_idx).astype(F32)
    i_row = jnp.full((1, blk), i, jnp.int32)
    km_hi, km_mid, km_lo = _split3(kmean_ref[...])
    cases, case = _extent_cases(i, seq)

    for hh in range(MOBA_HEADS_PER_STEP):
        head = (g * MOBA_HEADS_PER_STEP + hh + 1).astype(F32)
        slope = jnp.exp2(jnp.full((1, blk), -8.0 / N_HEADS, F32) * head)
        q_h = q_ref[:, hh * d:(hh + 1) * d]
        gate = (lax.dot_general(km_hi, q_h, _NT, preferred_element_type=F32)
                + lax.dot_general(km_mid, q_h, _NT, preferred_element_type=F32)
                + lax.dot_general(km_lo, q_h, _NT, preferred_element_type=F32))
        gate = jnp.where(past, gate, NEG_INF)
        rank = jnp.zeros((nb, blk), F32)
        for mth in range(nb):
            gm = gate[mth:mth + 1, :]
            beats = (gm > gate) | ((gm == gate) & (blk_idx > mth))
            rank = rank + jnp.where(beats, 1.0, 0.0)
        selected = (rank < TOP_K_A) & past
        pen_rows = jnp.where(selected, -(slope * blk) * blocks_back, NEG_INF)
        bias0 = slope * dist0

        for e, extent in cases:
            @pl.when(case == e)
            def _(q_h=q_h, pen_rows=pen_rows, bias0=bias0, extent=extent, hh=hh):
                s = lax.dot_general(kb_ref[0:extent, :], q_h, _NT, preferred_element_type=F32) * scale
                parts = []
                for n in range(extent // blk):
                    pen = jnp.where(i_row == n, causal_pen, pen_rows[n:n + 1, :])
                    parts.append(s[n * blk:(n + 1) * blk, :] - bias0 + pen)
                out = _softmax_av_t(jnp.concatenate(parts, axis=0), vt_ref[0, :, 0:extent])
                o_ref[:, hh * d:(hh + 1) * d] = out.astype(o_ref.dtype)


def _moba_prompt(q2d, k, vt, batch, seq):
    assert seq % BLOCK_A == 0 and Q_TILE == BLOCK_A and N_HEADS % MOBA_HEADS_PER_STEP == 0
    nb = seq // BLOCK_A
    hd = N_HEADS * HEAD_DIM_A
    wq = MOBA_HEADS_PER_STEP * HEAD_DIM_A
    return pl.pallas_call(
        functools.partial(_moba_prompt_kernel, nb=nb, scale=HEAD_DIM_A ** -0.5),
        grid=(batch, nb, N_HEADS // MOBA_HEADS_PER_STEP),
        in_specs=[
            pl.BlockSpec((BLOCK_A, wq), lambda b, i, g: (b * nb + i, g)),
            pl.BlockSpec((1, seq, HEAD_DIM_A), lambda b, i, g: (b, 0, 0)),
            pl.BlockSpec((1, HEAD_DIM_A, seq), lambda b, i, g: (b, 0, 0)),
        ],
        out_specs=pl.BlockSpec((BLOCK_A, wq), lambda b, i, g: (b * nb + i, g)),
        out_shape=jax.ShapeDtypeStruct((batch * seq, hd), BF16),
        scratch_shapes=[
            pltpu.VMEM((seq, HEAD_DIM_A), BF16),
            pltpu.VMEM((nb, HEAD_DIM_A), F32),
        ],
        compiler_params=_params(("parallel", "arbitrary", "arbitrary")),
        name="moba_prompt",
    )(q2d, k, vt)


def _mla_prompt_kernel(qn_ref, qr_ref, kn_ref, kr_ref, vt_ref, o_ref, *, seq, scale):
    i = pl.program_id(2)
    t = Q_TILE
    q = jnp.concatenate([qn_ref[...], qr_ref[...]], axis=1)
    tok_pos = i * t + lax.broadcasted_iota(jnp.int32, (1, t), 1)
    cases, case = _extent_cases(i, seq)
    for e, extent in cases:
        @pl.when(case == e)
        def _(extent=extent):
            keys = jnp.concatenate([kn_ref[0:extent, :], kr_ref[0, 0, 0:extent, :]], axis=1)
            s = lax.dot_general(keys, q, _NT, preferred_element_type=F32) * scale
            key_pos = lax.broadcasted_iota(jnp.int32, (extent, t), 0)
            s = jnp.where(key_pos <= tok_pos, s, NEG_INF)
            o_ref[...] = _softmax_av_t(s, vt_ref[:, 0:extent]).astype(o_ref.dtype)


def _mla_prompt(qn, qr, kn, kr_exp, vt, batch, seq):
    t = Q_TILE
    assert seq % t == 0
    nq = seq // t
    heads_per_lane_group = LANES // ROPE_DIM
    return pl.pallas_call(
        functools.partial(_mla_prompt_kernel, seq=seq, scale=(NOPE_DIM + ROPE_DIM) ** -0.5),
        grid=(batch, N_HEADS, nq),
        in_specs=[
            pl.BlockSpec((t, NOPE_DIM), lambda b, h, i: (b * nq + i, h)),
            pl.BlockSpec((t, LANES), lambda b, h, i: (b * nq + i, h // heads_per_lane_group)),
            pl.BlockSpec((seq, NOPE_DIM), lambda b, h, i: (b, h)),
            pl.BlockSpec((1, 1, seq, LANES), lambda b, h, i: (b, h % heads_per_lane_group, 0, 0)),
            pl.BlockSpec((V_DIM_B, seq), lambda b, h, i: (h, b)),
        ],
        out_specs=pl.BlockSpec((t, V_DIM_B), lambda b, h, i: (b * nq + i, h)),
        out_shape=jax.ShapeDtypeStruct((batch * seq, N_HEADS * V_DIM_B), BF16),
        compiler_params=_params(("parallel", "parallel", "arbitrary")),
        name="mla_prompt",
    )(qn, qr, kn, kr_exp, vt)


CHUNK_PAGES = 16


def _chunk_copies(page_of, specs, slot):
    out = []
    for p in range(CHUNK_PAGES):
        page = page_of(p)
        for hbm, layer, buf, sem, transposed in specs:
            rows = pl.ds(p * PAGE_SIZE, PAGE_SIZE)
            dst = buf.at[slot, :, rows] if transposed else buf.at[slot, rows]
            out.append(pltpu.make_async_copy(hbm.at[layer, page], dst, sem.at[slot]))
    return out


def _stream_chunks(pt_ref, b, n_seq, nchunks, specs, phase, init):
    def start(seq, c, slot):
        for cpy in _chunk_copies(lambda p: pt_ref[seq, c * CHUNK_PAGES + p], specs, slot):
            cpy.start()

    def wait(slot):
        for cpy in _chunk_copies(lambda p: 0, specs, slot):
            cpy.wait()

    @pl.when(b == 0)
    def _():
        start(0, 0, 0)

    def pair(k, carry):
        c0 = 2 * k
        wait(0)
        start(b, c0 + 1, 1)
        carry = phase(c0, 0, carry)
        wait(1)
        last = c0 + 2 == nchunks

        @pl.when(jnp.logical_or(jnp.logical_not(last), b + 1 < n_seq))
        def _():
            start(jnp.where(last, b + 1, b), jnp.where(last, 0, c0 + 2), 0)

        return phase(c0 + 1, 1, carry)

    return lax.fori_loop(0, nchunks // 2, pair, init)


def _mla_decode_kernel(pt_ref, ql_ref, qr_ref, cnew_ref, rnew_ref, ckv_hbm, krt_hbm, o_ref,
                       cbuf, rbuf, csem, rsem, kb_ref, p_ref, *, layer, n_pages, n_seq, scale):
    b = pl.program_id(0)
    nchunks = n_pages // CHUNK_PAGES
    specs = [(ckv_hbm, layer, cbuf, csem, False), (krt_hbm, layer, rbuf, rsem, True)]
    ql = ql_ref[0]
    qr = qr_ref[0]

    @pl.when(b == 0)
    def _():
        kb_ref[1] = jnp.zeros(kb_ref.shape[1:], BF16)
        p_ref[1] = jnp.zeros(p_ref.shape[1:], BF16)

    def deferred_av(slot):
        return jnp.dot(p_ref[slot], kb_ref[slot], preferred_element_type=F32)

    def phase(c, slot, carry):
        m, l, acc, a_prev = carry
        acc = acc * a_prev + deferred_av(1 - slot)
        kc = cbuf[slot].astype(BF16)
        kb_ref[slot] = kc
        kr = rbuf[slot].astype(BF16)
        s = (lax.dot_general(ql, kc, _NT, preferred_element_type=F32)
             + jnp.dot(qr, kr, preferred_element_type=F32)) * scale
        m_new = jnp.maximum(m, jnp.max(s, axis=1, keepdims=True))
        a = jnp.exp(m - m_new)
        p = jnp.exp(s - m_new)
        p_ref[slot] = p.astype(BF16)
        return m_new, a * l + jnp.sum(p, axis=1, keepdims=True), acc, a

    init = (jnp.full((N_HEADS, 1), NEG_INF, F32), jnp.zeros((N_HEADS, 1), F32),
            jnp.zeros((N_HEADS, KV_LORA), F32), jnp.zeros((N_HEADS, 1), F32))
    m, l, acc, a_prev = _stream_chunks(pt_ref, b, n_seq, nchunks, specs, phase, init)
    acc = acc * a_prev + deferred_av(1)
    cnew = cnew_ref[0]
    rnew = rnew_ref[0]
    s_new = (jnp.sum(ql.astype(F32) * cnew, axis=1, keepdims=True)
             + jnp.sum(qr.astype(F32) * rnew, axis=1, keepdims=True)) * scale
    m_f = jnp.maximum(m, s_new)
    a = jnp.exp(m - m_f)
    pn = jnp.exp(s_new - m_f)
    o_ref[0] = (a * acc + pn * cnew) / (a * l + pn)


def _mla_decode(page_table, q_lat, q_rope, ckv_new, kr_new, cache_ckv, cache_kr_t, layer):
    db, n_pages = page_table.shape
    assert n_pages % (2 * CHUNK_PAGES) == 0
    rows = CHUNK_PAGES * PAGE_SIZE
    seq3 = lambda b, pt: (b, 0, 0)
    grid_spec = pltpu.PrefetchScalarGridSpec(
        num_scalar_prefetch=1,
        grid=(db,),
        in_specs=[
            pl.BlockSpec((1, N_HEADS, KV_LORA), seq3),
            pl.BlockSpec((1, N_HEADS, ROPE_DIM), seq3),
            pl.BlockSpec((1, 1, KV_LORA), seq3),
            pl.BlockSpec((1, 1, ROPE_DIM), seq3),
            pl.BlockSpec(memory_space=pl.ANY),
            pl.BlockSpec(memory_space=pl.ANY),
        ],
        out_specs=pl.BlockSpec((1, N_HEADS, KV_LORA), seq3),
        scratch_shapes=[
            pltpu.VMEM((2, rows, KV_LORA), F32),
            pltpu.VMEM((2, ROPE_DIM, rows), F32),
            pltpu.SemaphoreType.DMA((2,)),
            pltpu.SemaphoreType.DMA((2,)),
            pltpu.VMEM((2, rows, KV_LORA), BF16),
            pltpu.VMEM((2, N_HEADS, rows), BF16),
        ],
    )
    return pl.pallas_call(
        functools.partial(_mla_decode_kernel, layer=layer, n_pages=n_pages, n_seq=db,
                          scale=(NOPE_DIM + ROPE_DIM) ** -0.5),
        grid_spec=grid_spec,
        out_shape=jax.ShapeDtypeStruct((db, N_HEADS, KV_LORA), F32),
        compiler_params=_params(("arbitrary",)),
        name="mla_decode",
    )(page_table, q_lat, q_rope, ckv_new, kr_new, cache_ckv, cache_kr_t)


def _moba_decode_kernel(pt_ref, q_ref, knew_ref, vnew_ref, k_hbm, v_hbm, o_ref,
                        kbuf, vbuf, ksem, vsem, kmean_ref, m_ref, l_ref, acc_ref, vb_ref, p_ref,
                        *, layer, n_pages, n_seq, scale):
    b = pl.program_id(0)
    nchunks = n_pages // CHUNK_PAGES
    blk = BLOCK_A
    d = HEAD_DIM_A
    rows = CHUNK_PAGES * PAGE_SIZE
    bpc = rows // blk
    nblk = n_pages * PAGE_SIZE // blk
    nblk_pad = kmean_ref.shape[0]
    past = n_pages * PAGE_SIZE
    specs = [(k_hbm, layer, kbuf, ksem, False), (v_hbm, layer, vbuf, vsem, False)]

    q = q_ref[0]
    head = lax.broadcasted_iota(jnp.int32, (N_HEADS, 1), 0).astype(F32)
    slopes = jnp.exp2(-8.0 * (head + 1.0) / N_HEADS)
    key_off = lax.broadcasted_iota(jnp.int32, (1, rows), 1).astype(F32)
    if nblk_pad > nblk:
        kmean_ref[nblk:, :] = jnp.zeros((nblk_pad - nblk, d), F32)

    @pl.when(b == 0)
    def _():
        vb_ref[1] = jnp.zeros(vb_ref.shape[1:], BF16)
        p_ref[1] = jnp.zeros(p_ref.shape[1:], BF16)

    def deferred_av(slot, first_block):
        accs = [jnp.dot(p_ref[slot, :, j * blk:(j + 1) * blk], vb_ref[slot, j * blk:(j + 1) * blk, :],
                        preferred_element_type=F32) for j in range(bpc)]
        acc_ref[pl.ds(first_block, bpc)] = jnp.stack(accs)

    def phase(c, slot, carry):
        deferred_av(1 - slot, jnp.where(c == 0, nblk, (c - 1) * bpc))
        kc = kbuf[slot]
        first = pl.multiple_of(c * bpc, bpc)
        kmean_ref[pl.ds(first, bpc), :] = jnp.mean(kc.reshape(bpc, blk, d), axis=1)
        vb_ref[slot] = vbuf[slot].astype(BF16)
        dist = lax.convert_element_type(past - c * rows, F32) - key_off
        s = lax.dot_general(q, kc.astype(BF16), _NT, preferred_element_type=F32) * scale - slopes * dist
        ms, ls, ps = [], [], []
        for j in range(bpc):
            sj = s[:, j * blk:(j + 1) * blk]
            mj = jnp.max(sj, axis=1, keepdims=True)
            pj = jnp.exp(sj - mj)
            ps.append(pj.astype(BF16))
            ls.append(jnp.broadcast_to(jnp.sum(pj, axis=1, keepdims=True), (N_HEADS, d)))
            ms.append(jnp.broadcast_to(mj, (N_HEADS, d)))
        p_ref[slot] = jnp.concatenate(ps, axis=1)
        m_ref[pl.ds(first, bpc)] = jnp.stack(ms)
        l_ref[pl.ds(first, bpc)] = jnp.stack(ls)
        return carry

    _stream_chunks(pt_ref, b, n_seq, nchunks, specs, phase, 0)
    deferred_av(1, (nchunks - 1) * bpc)

    km_hi, km_mid, km_lo = _split3(kmean_ref[...])
    gate = (lax.dot_general(q, km_hi, _NT, preferred_element_type=F32)
            + lax.dot_general(q, km_mid, _NT, preferred_element_type=F32)
            + lax.dot_general(q, km_lo, _NT, preferred_element_type=F32))
    idx = lax.broadcasted_iota(jnp.int32, (N_HEADS, nblk_pad), 1).astype(F32)
    gate = jnp.where(idx < nblk, gate, NEG_INF)
    sel = jnp.zeros((N_HEADS, nblk_pad), F32)
    for _ in range(min(TOP_K_A, nblk)):
        mx = jnp.max(gate, axis=1, keepdims=True)
        first = jnp.min(jnp.where(gate == mx, idx, float(nblk_pad)), axis=1, keepdims=True)
        pick = idx == first
        sel = jnp.where(pick, 1.0, sel)
        gate = jnp.where(pick, NEG_INF, gate)

    knew = knew_ref[0]
    vnew = vnew_ref[0]
    s_own = jnp.sum(q.astype(F32) * knew, axis=1, keepdims=True) * scale
    m_all = jnp.broadcast_to(s_own, (N_HEADS, d))
    sel_b = []
    for n in range(nblk):
        sn = jnp.broadcast_to(sel[:, n:n + 1], (N_HEADS, d)) > 0.0
        sel_b.append(sn)
        m_all = jnp.maximum(m_all, jnp.where(sn, m_ref[n], NEG_INF))
    w_own = jnp.exp(jnp.broadcast_to(s_own, (N_HEADS, d)) - m_all)
    l_all = w_own
    acc_all = w_own * vnew
    for n in range(nblk):
        w = jnp.where(sel_b[n], jnp.exp(m_ref[n] - m_all), 0.0)
        l_all = l_all + w * l_ref[n]
        acc_all = acc_all + w * acc_ref[n]
    o_ref[0] = acc_all / l_all


def _moba_decode(page_table, q, k_new, v_new, cache_k, cache_v, layer):
    db, n_pages = page_table.shape
    assert n_pages % (2 * CHUNK_PAGES) == 0 and (CHUNK_PAGES * PAGE_SIZE) % BLOCK_A == 0
    rows = CHUNK_PAGES * PAGE_SIZE
    nblk = n_pages * PAGE_SIZE // BLOCK_A
    nblk_pad = -(-nblk // LANES) * LANES
    d = HEAD_DIM_A
    seq3 = lambda b, pt: (b, 0, 0)
    grid_spec = pltpu.PrefetchScalarGridSpec(
        num_scalar_prefetch=1,
        grid=(db,),
        in_specs=[
            pl.BlockSpec((1, N_HEADS, d), seq3),
            pl.BlockSpec((1, 1, d), seq3),
            pl.BlockSpec((1, 1, d), seq3),
            pl.BlockSpec(memory_space=pl.ANY),
            pl.BlockSpec(memory_space=pl.ANY),
        ],
        out_specs=pl.BlockSpec((1, N_HEADS, d), seq3),
        scratch_shapes=[
            pltpu.VMEM((2, rows, d), F32),
            pltpu.VMEM((2, rows, d), F32),
            pltpu.SemaphoreType.DMA((2,)),
            pltpu.SemaphoreType.DMA((2,)),
            pltpu.VMEM((nblk_pad, d), F32),
            pltpu.VMEM((nblk, N_HEADS, d), F32),
            pltpu.VMEM((nblk, N_HEADS, d), F32),
            pltpu.VMEM((nblk + rows // BLOCK_A, N_HEADS, d), F32),
            pltpu.VMEM((2, rows, d), BF16),
            pltpu.VMEM((2, N_HEADS, rows), BF16),
        ],
    )
    return pl.pallas_call(
        functools.partial(_moba_decode_kernel, layer=layer, n_pages=n_pages, n_seq=db,
                          scale=HEAD_DIM_A ** -0.5),
        grid_spec=grid_spec,
        out_shape=jax.ShapeDtypeStruct((db, N_HEADS, d), F32),
        compiler_params=_params(("arbitrary",)),
        name="moba_decode",
    )(page_table, q, k_new, v_new, cache_k, cache_v)


def _rope_tables(pos):
    half = ROPE_DIM // 2
    inv = jnp.power(ROPE_THETA, -2.0 * jnp.arange(half, dtype=F32) / ROPE_DIM)
    ang = pos.astype(F32)[:, None] * inv[None, :]
    cos, sin = jnp.cos(ang), jnp.sin(ang)
    return jnp.concatenate([cos, cos], -1), jnp.concatenate([-sin, sin], -1)


def _layer_weights(l, w_in, w_uq, g_qn, g_kvn, w_uk, w_uv, w_oa, w_ob, w_out, ln1_g, ln1_b,
                   w_up, w_down, ln2_g, ln2_b):
    dm = w_in.shape[1]
    win_t = jnp.swapaxes(w_in[l], 0, 1)
    hd = N_HEADS * HEAD_DIM_A
    o_k, o_cq = hd, hd + 2 * HEAD_DIM_A
    o_ckv = o_cq + Q_LORA
    o_kr = o_ckv + KV_LORA
    o_g = o_kr + ROPE_DIM
    half = ROPE_DIM // 2
    swap = jnp.concatenate([jnp.arange(half, ROPE_DIM), jnp.arange(half)])
    kr_rows = win_t[o_kr:o_g]
    pad = jnp.zeros((LANES - ROPE_DIM, dm), F32)
    w_kv_t = jnp.concatenate([win_t[o_k:o_cq], win_t[o_ckv:o_kr], kr_rows, pad, kr_rows[swap], pad], axis=0)
    wuq = w_uq[l].reshape(Q_LORA, N_HEADS, NOPE_DIM + ROPE_DIM)
    rope_cols = wuq[:, :, NOPE_DIM:]
    w_q = jnp.concatenate([
        wuq[:, :, :NOPE_DIM].reshape(Q_LORA, N_HEADS * NOPE_DIM),
        rope_cols.reshape(Q_LORA, N_HEADS * ROPE_DIM),
        rope_cols[:, :, swap].reshape(Q_LORA, N_HEADS * ROPE_DIM)], axis=1)
    return dict(
        w_qc_t=jnp.concatenate([win_t[:hd], win_t[o_cq:o_ckv]], axis=0).astype(BF16),
        w_kv_t=w_kv_t.astype(BF16),
        w_gate_t=win_t[o_g:].astype(BF16),
        w_q=w_q.astype(BF16),
        w_kup=w_uk[l].reshape(KV_LORA, N_HEADS * NOPE_DIM).astype(BF16),
        w_uv_t=w_uv[l].reshape(KV_LORA, N_HEADS * V_DIM_B).T.astype(BF16),
        w_uk_h=jnp.transpose(w_uk[l], (1, 2, 0)).astype(BF16),
        w_uv_h=jnp.transpose(w_uv[l], (1, 0, 2)).astype(BF16),
        w_oa=w_oa[l].astype(BF16), w_ob=w_ob[l].astype(BF16), w_out=w_out[l].astype(BF16),
        w_up=w_up[l].astype(BF16), w_down=w_down[l].astype(BF16),
        g_qn=g_qn[l][None, :], g_kvn=g_kvn[l][None, :],
        ln1_g=ln1_g[l][None, :], ln1_b=ln1_b[l][None, :], ln2_g=ln2_g[l][None, :], ln2_b=ln2_b[l][None, :],
    )


def _project(xb, w, cos_k, sin_k, cos_q, sin_q):
    qc = _matmul(xb, w["w_qc_t"], out_dtype=BF16, nt=True, name="proj_q")
    k_a, v_a, ckv, krope = _kv_project(xb, w["w_kv_t"], w["g_kvn"], cos_k, sin_k)
    gates = _matmul(xb, w["w_gate_t"], out_dtype=BF16, act="sigmoid", nt=True, name="proj_gates")
    qn, qr = _q_project(qc, w["g_qn"], w["w_q"], cos_q, sin_q)
    return qc, k_a, v_a, ckv, krope, gates, qn, qr


def _finish(x, att_a, att_b, gates, w, alpha):
    mixin = _gated_pair_matmul(att_a, w["w_oa"], att_b, w["w_ob"], gates)
    mix = _matmul(mixin, w["w_out"], out_dtype=F32, name="w_out")
    x1, x1b = _residual_layernorm(x, mix, w["ln1_g"], w["ln1_b"], alpha)
    hdn = _matmul(x1b, w["w_up"], out_dtype=BF16, act="relu2", name="ffn_up")
    ff = _matmul(hdn, w["w_down"], out_dtype=F32, bk=2048, name="ffn_down")
    return _residual_layernorm(x1, ff, w["ln2_g"], w["ln2_b"], alpha)


def kernel(x_prompt, x_sample, cache_moba_k, cache_moba_v, cache_mla_ckv, cache_mla_krope, page_table, w_in, w_uq, g_qn, g_kvn, w_uk, w_uv, w_oa, w_ob, w_out, ln1_g, ln1_b, w_up, w_down, ln2_g, ln2_b):
    depth, dm, _ = w_in.shape
    batch, seq, _ = x_prompt.shape
    db, dec_seq, _ = x_sample.shape
    n_pages = page_table.shape[1]
    past = n_pages * PAGE_SIZE
    assert dec_seq == 1 and past % BLOCK_A == 0 and cache_moba_k.shape[2] == PAGE_SIZE
    alpha = float((2 * depth) ** 0.25)
    hd = N_HEADS * HEAD_DIM_A
    p_rows = batch * seq

    cos_p, sin_p = _rope_tables(jnp.arange(seq))
    cos_s, sin_s = _rope_tables(jnp.full((db,), past))
    cos_kp, sin_kp = jnp.tile(cos_p, (batch, 1)), jnp.tile(sin_p, (batch, 1))
    cos_qp, sin_qp = jnp.tile(cos_p, (1, N_HEADS)), jnp.tile(sin_p, (1, N_HEADS))
    cos_qs, sin_qs = jnp.tile(cos_s, (1, N_HEADS)), jnp.tile(sin_s, (1, N_HEADS))
    lane_groups = LANES // ROPE_DIM
    cache_kr_t = jnp.swapaxes(cache_mla_krope, 2, 3)

    xp = x_prompt.reshape(p_rows, dm)
    xs = x_sample.reshape(db, dm)
    xpb, xsb = xp.astype(BF16), xs.astype(BF16)
    new_p, new_s = [], []
    for l in range(depth):
        w = _layer_weights(l, w_in, w_uq, g_qn, g_kvn, w_uk, w_uv, w_oa, w_ob, w_out, ln1_g, ln1_b,
                           w_up, w_down, ln2_g, ln2_b)
        qc, k_a, v_a, ckv, krope, gates, qn, qr = _project(xpb, w, cos_kp, sin_kp, cos_qp, sin_qp)
        k3 = k_a.reshape(batch, seq, HEAD_DIM_A)
        vt_a = jnp.swapaxes(v_a.reshape(batch, seq, HEAD_DIM_A), 1, 2).astype(BF16)
        att_a = _moba_prompt(qc, k3, vt_a, batch, seq)
        ckv_b = ckv.astype(BF16)
        k_nope = _matmul(ckv_b, w["w_kup"], out_dtype=BF16, name="k_up")
        vt_b = _matmul(w["w_uv_t"], ckv_b, out_dtype=BF16, nt=True, name="v_up_t")
        kr3 = krope.reshape(batch, seq, ROPE_DIM).astype(BF16)
        kr_exp = jnp.stack([jnp.pad(kr3, ((0, 0), (0, 0), (g * ROPE_DIM, LANES - (g + 1) * ROPE_DIM)))
                            for g in range(lane_groups)], axis=1)
        att_b = _mla_prompt(qn, qr, k_nope, kr_exp, vt_b, batch, seq)
        xp, xpb = _finish(xp, att_a, att_b, gates, w, alpha)
        new_p.append((k3, v_a.reshape(batch, seq, -1), ckv.reshape(batch, seq, -1), krope.reshape(batch, seq, -1)))

        qc, k_a, v_a, ckv, krope, gates, qn, qr = _project(xsb, w, cos_s, sin_s, cos_qs, sin_qs)
        q_a = qc[:, :hd].reshape(db, N_HEADS, HEAD_DIM_A)
        att_a = _moba_decode(page_table, q_a, k_a.reshape(db, 1, -1), v_a.reshape(db, 1, -1),
                             cache_moba_k, cache_moba_v, l)
        q_lat = _head_matmul(qn, w["w_uk_h"], BF16).reshape(db, N_HEADS, KV_LORA)
        att_lat = _mla_decode(page_table, q_lat, qr.reshape(db, N_HEADS, ROPE_DIM),
                              ckv.reshape(db, 1, -1), krope.reshape(db, 1, -1),
                              cache_mla_ckv, cache_kr_t, l)
        att_b = _head_matmul(att_lat.reshape(db, N_HEADS * KV_LORA), w["w_uv_h"], BF16)
        xs, xsb = _finish(xs, att_a.reshape(db, hd).astype(BF16), att_b, gates, w, alpha)
        new_s.append((k_a.reshape(db, 1, -1), v_a.reshape(db, 1, -1),
                      ckv.reshape(db, 1, -1), krope.reshape(db, 1, -1)))

    stack = lambda items, j: jnp.stack([it[j] for it in items])
    return (xp.reshape(batch, seq, dm), xs.reshape(db, dec_seq, dm),
            stack(new_p, 0), stack(new_p, 1), stack(new_p, 2), stack(new_p, 3),
            stack(new_s, 0), stack(new_s, 1), stack(new_s, 2), stack(new_s, 3))
```

```python
import functools

import jax
import jax.numpy as jnp
from jax import lax
from jax.experimental import pallas as pl
from jax.experimental.pallas import tpu as pltpu

F32 = jnp.float32
BF16 = jnp.bfloat16

N_HEADS = 16
HEAD_DIM_A = 128
BLOCK_A = 256
TOP_K_A = 3
NOPE_DIM = 128
ROPE_DIM = 32
V_DIM_B = 128
Q_LORA = 1024
KV_LORA = 256
ROPE_THETA = 10000.0
PAGE_SIZE = 128
LN_EPS = 1e-5
RMS_EPS = 1e-6
LANES = 128
VMEM_LIMIT = 56 * 1024 * 1024
NEG_INF = float("-inf")

_NT = (((1,), (1,)), ((), ()))


def _pick(dim, target, align):
    best = None
    d = align
    while d <= min(dim, target):
        if dim % d == 0:
            best = d
        d += align
    return best if best is not None else dim


def _params(sem):
    return pltpu.CompilerParams(dimension_semantics=sem, vmem_limit_bytes=VMEM_LIMIT)


def _act(name, v):
    if name == "sigmoid":
        return jax.nn.sigmoid(v)
    if name == "relu2":
        r = jnp.maximum(v, 0.0)
        return r * r
    return v


def _mm_kernel(x_ref, w_ref, o_ref, *scratch, nk, act, nt):
    if nt:
        part = lax.dot_general(x_ref[...], w_ref[...], _NT, preferred_element_type=F32)
    else:
        part = jnp.dot(x_ref[...], w_ref[...], preferred_element_type=F32)
    if nk == 1:
        o_ref[...] = _act(act, part).astype(o_ref.dtype)
        return
    (acc_ref,) = scratch
    k = pl.program_id(2)

    @pl.when(k == 0)
    def _():
        acc_ref[...] = part

    @pl.when(k > 0)
    def _():
        acc_ref[...] += part

    @pl.when(k == nk - 1)
    def _():
        o_ref[...] = _act(act, acc_ref[...]).astype(o_ref.dtype)


def _matmul(x, w, *, out_dtype, act=None, nt=False, bm=1024, bn=1024, bk=4096, name="matmul"):
    m, kdim = x.shape
    n = w.shape[0] if nt else w.shape[1]
    bm = _pick(m, bm, 8)
    bn = _pick(n, bn, LANES)
    bk = _pick(kdim, bk, LANES)
    nk = kdim // bk
    w_spec = (pl.BlockSpec((bn, bk), lambda i, j, k: (j, k)) if nt
              else pl.BlockSpec((bk, bn), lambda i, j, k: (k, j)))
    return pl.pallas_call(
        functools.partial(_mm_kernel, nk=nk, act=act, nt=nt),
        grid=(m // bm, n // bn, nk),
        in_specs=[pl.BlockSpec((bm, bk), lambda i, j, k: (i, k)), w_spec],
        out_specs=pl.BlockSpec((bm, bn), lambda i, j, k: (i, j)),
        out_shape=jax.ShapeDtypeStruct((m, n), out_dtype),
        scratch_shapes=[pltpu.VMEM((bm, bn), F32)] if nk > 1 else [],
        compiler_params=_params(("parallel", "parallel", "arbitrary")),
        name=name,
    )(x, w)


def _gmm2_kernel(a_ref, wa_ref, b_ref, wb_ref, ga_ref, gb_ref, o_ref):
    oa = jnp.dot(a_ref[...], wa_ref[...], preferred_element_type=F32)
    ob = jnp.dot(b_ref[...], wb_ref[...], preferred_element_type=F32)
    o_ref[...] = (ga_ref[...].astype(F32) * oa + gb_ref[...].astype(F32) * ob).astype(o_ref.dtype)


def _gated_pair_matmul(a, wa, b, wb, gates):
    m, ka = a.shape
    kb = b.shape[1]
    n = wa.shape[1]
    bm = _pick(m, 1024, 8)
    bn = _pick(n, 1024, LANES)
    nj = n // bn
    return pl.pallas_call(
        _gmm2_kernel,
        grid=(m // bm, nj),
        in_specs=[
            pl.BlockSpec((bm, ka), lambda i, j: (i, 0)),
            pl.BlockSpec((ka, bn), lambda i, j: (0, j)),
            pl.BlockSpec((bm, kb), lambda i, j: (i, 0)),
            pl.BlockSpec((kb, bn), lambda i, j: (0, j)),
            pl.BlockSpec((bm, bn), lambda i, j: (i, j)),
            pl.BlockSpec((bm, bn), lambda i, j: (i, nj + j)),
        ],
        out_specs=pl.BlockSpec((bm, bn), lambda i, j: (i, j)),
        out_shape=jax.ShapeDtypeStruct((m, n), BF16),
        compiler_params=_params(("parallel", "parallel")),
        name="gated_pair_matmul",
    )(a, wa, b, wb, gates, gates)


def _headmm_kernel(x_ref, w_ref, o_ref):
    o_ref[...] = jnp.dot(x_ref[...].astype(BF16), w_ref[0], preferred_element_type=F32).astype(o_ref.dtype)


def _head_matmul(x, w, out_dtype):
    m = x.shape[0]
    nh, kh, nn = w.shape
    return pl.pallas_call(
        _headmm_kernel,
        grid=(nh,),
        in_specs=[pl.BlockSpec((m, kh), lambda h: (0, h)), pl.BlockSpec((1, kh, nn), lambda h: (h, 0, 0))],
        out_specs=pl.BlockSpec((m, nn), lambda h: (0, h)),
        out_shape=jax.ShapeDtypeStruct((m, nh * nn), out_dtype),
        compiler_params=_params(("parallel",)),
        name="head_matmul",
    )(x, w)


def _kvproj_kernel(x_ref, w_ref, g_ref, cos_ref, sin_ref, k_ref, v_ref, ckv_ref, kr_ref):
    h = lax.dot_general(x_ref[...], w_ref[...], _NT, preferred_element_type=F32)
    d = HEAD_DIM_A
    k_ref[...] = h[:, 0:d]
    v_ref[...] = h[:, d:2 * d]
    c = h[:, 2 * d:2 * d + KV_LORA]
    ckv_ref[...] = c * lax.rsqrt(jnp.mean(c * c, axis=-1, keepdims=True) + RMS_EPS) * g_ref[...]
    o = 2 * d + KV_LORA
    r = h[:, o:o + ROPE_DIM]
    r_swapped = h[:, o + LANES:o + LANES + ROPE_DIM]
    kr_ref[...] = r * cos_ref[...] + r_swapped * sin_ref[...]


def _kv_project(xb, w_kv_t, g_kvn, cos_k, sin_k):
    m, dm = xb.shape
    bm = _pick(m, 1024, 8)
    nw = w_kv_t.shape[0]
    row = lambda i: (i, 0)
    return pl.pallas_call(
        _kvproj_kernel,
        grid=(m // bm,),
        in_specs=[
            pl.BlockSpec((bm, dm), row),
            pl.BlockSpec((nw, dm), lambda i: (0, 0)),
            pl.BlockSpec((1, KV_LORA), lambda i: (0, 0)),
            pl.BlockSpec((bm, ROPE_DIM), row),
            pl.BlockSpec((bm, ROPE_DIM), row),
        ],
        out_specs=[
            pl.BlockSpec((bm, HEAD_DIM_A), row),
            pl.BlockSpec((bm, HEAD_DIM_A), row),
            pl.BlockSpec((bm, KV_LORA), row),
            pl.BlockSpec((bm, ROPE_DIM), row),
        ],
        out_shape=[
            jax.ShapeDtypeStruct((m, HEAD_DIM_A), F32),
            jax.ShapeDtypeStruct((m, HEAD_DIM_A), F32),
            jax.ShapeDtypeStruct((m, KV_LORA), F32),
            jax.ShapeDtypeStruct((m, ROPE_DIM), F32),
        ],
        compiler_params=_params(("parallel",)),
        name="kv_project",
    )(xb, w_kv_t, g_kvn, cos_k, sin_k)


def _qproj_kernel(c_ref, g_ref, w_ref, cos_ref, sin_ref, qn_ref, qr_ref):
    c = c_ref[...].astype(F32)
    cn = (c * lax.rsqrt(jnp.mean(c * c, axis=-1, keepdims=True) + RMS_EPS) * g_ref[...]).astype(BF16)
    z = jnp.dot(cn, w_ref[...], preferred_element_type=F32)
    n_nope = N_HEADS * NOPE_DIM
    n_rope = N_HEADS * ROPE_DIM
    qn_ref[...] = z[:, :n_nope].astype(qn_ref.dtype)
    rope = z[:, n_nope:n_nope + n_rope] * cos_ref[...] + z[:, n_nope + n_rope:] * sin_ref[...]
    qr_ref[...] = rope.astype(qr_ref.dtype)


def _q_project(qc, g_qn, w_q, cos_q, sin_q):
    m = qc.shape[0]
    period = cos_q.shape[0]
    bm = _pick(period, 512, 8)
    nw = w_q.shape[1]
    n_nope = N_HEADS * NOPE_DIM
    n_rope = N_HEADS * ROPE_DIM
    c_blk = (N_HEADS * HEAD_DIM_A) // Q_LORA
    row = lambda i: (i, 0)
    tab = lambda i: (i % (period // bm), 0)
    return pl.pallas_call(
        _qproj_kernel,
        grid=(m // bm,),
        in_specs=[
            pl.BlockSpec((bm, Q_LORA), lambda i: (i, c_blk)),
            pl.BlockSpec((1, Q_LORA), lambda i: (0, 0)),
            pl.BlockSpec((Q_LORA, nw), lambda i: (0, 0)),
            pl.BlockSpec((bm, n_rope), tab),
            pl.BlockSpec((bm, n_rope), tab),
        ],
        out_specs=[pl.BlockSpec((bm, n_nope), row), pl.BlockSpec((bm, n_rope), row)],
        out_shape=[jax.ShapeDtypeStruct((m, n_nope), BF16), jax.ShapeDtypeStruct((m, n_rope), BF16)],
        compiler_params=_params(("parallel",)),
        name="q_project",
    )(qc, g_qn, w_q, cos_q, sin_q)


def _ln_kernel(x_ref, y_ref, g_ref, b_ref, o_ref, ob_ref, *, alpha):
    z = alpha * x_ref[...] + y_ref[...]
    mu = jnp.mean(z, axis=-1, keepdims=True)
    zc = z - mu
    var = jnp.mean(zc * zc, axis=-1, keepdims=True)
    o = zc * lax.rsqrt(var + LN_EPS) * g_ref[...] + b_ref[...]
    o_ref[...] = o
    ob_ref[...] = o.astype(ob_ref.dtype)


def _residual_layernorm(x, y, g, b, alpha):
    m, dm = x.shape
    bm = _pick(m, 256, 8)
    row = lambda i: (i, 0)
    vec = lambda i: (0, 0)
    return pl.pallas_call(
        functools.partial(_ln_kernel, alpha=alpha),
        grid=(m // bm,),
        in_specs=[pl.BlockSpec((bm, dm), row), pl.BlockSpec((bm, dm), row),
                  pl.BlockSpec((1, dm), vec), pl.BlockSpec((1, dm), vec)],
        out_specs=[pl.BlockSpec((bm, dm), row), pl.BlockSpec((bm, dm), row)],
        out_shape=[jax.ShapeDtypeStruct((m, dm), F32), jax.ShapeDtypeStruct((m, dm), BF16)],
        compiler_params=_params(("parallel",)),
        name="residual_layernorm",
    )(x, y, g, b)


Q_TILE = 256
EXTENT_STEP = 512
MOBA_HEADS_PER_STEP = 4
MLA_HEADS_PER_STEP = 2


def _split3(v):
    hi = v.astype(BF16)
    r1 = v - hi.astype(F32)
    mid = r1.astype(BF16)
    lo = (r1 - mid.astype(F32)).astype(BF16)
    return hi, mid, lo


def _softmax_av_t(s, vt):
    m = jnp.max(s, axis=0, keepdims=True)
    p = jnp.exp(s - m)
    l = jnp.sum(p, axis=0, keepdims=True)
    out_t = jnp.dot(vt, p.astype(BF16), preferred_element_type=F32)
    return (out_t / l).T


def _extent_cases(i, seq):
    n_cases = -(-seq // EXTENT_STEP)
    return [(e, min((e + 1) * EXTENT_STEP, seq)) for e in range(n_cases)], (i * Q_TILE) // EXTENT_STEP


def _moba_prompt_kernel(q_ref, k_ref, vt_ref, o_ref, kb_ref, kmean_ref, *, nb, scale):
    i = pl.program_id(1)
    g = pl.program_id(2)
    blk = BLOCK_A
    d = HEAD_DIM_A
    seq = nb * blk

    @pl.when((i == 0) & (g == 0))
    def _per_batch():
        for n in range(nb):
            kn = k_ref[0, n * blk:(n + 1) * blk, :]
            kmean_ref[n:n + 1, :] = jnp.mean(kn, axis=0, keepdims=True)
            kb_ref[n * blk:(n + 1) * blk, :] = kn.astype(BF16)

    key_j = lax.broadcasted_iota(jnp.int32, (blk, blk), 0)
    tok_t = lax.broadcasted_iota(jnp.int32, (blk, blk), 1)
    dist0 = (tok_t - key_j).astype(F32)
    causal_pen = jnp.where(tok_t >= key_j, 0.0, NEG_INF)
    blk_idx = lax.broadcasted_iota(jnp.int32, (nb, blk), 0)
    past = blk_idx < i
    blocks_back = (i - blk_idx).astype(F32)
    i_row = jnp.full((1, blk), i, jnp.int32)
    km_hi, km_mid, km_lo = _split3(kmean_ref[...])
    cases, case = _extent_cases(i, seq)

    per_head = []
    for hh in range(MOBA_HEADS_PER_STEP):
        head = lax.convert_element_type(g * MOBA_HEADS_PER_STEP + hh + 1, F32)
        slope = jnp.exp2(jnp.full((1, blk), -8.0 / N_HEADS, F32) * head)
        q_h = q_ref[:, hh * d:(hh + 1) * d]
        gate = (lax.dot_general(km_hi, q_h, _NT, preferred_element_type=F32)
                + lax.dot_general(km_mid, q_h, _NT, preferred_element_type=F32)
                + lax.dot_general(km_lo, q_h, _NT, preferred_element_type=F32))
        gate = jnp.where(past, gate, NEG_INF)
        rank = jnp.zeros((nb, blk), F32)
        for mth in range(nb):
            gm = gate[mth:mth + 1, :]
            beats = (gm > gate) | ((gm == gate) & (blk_idx > mth))
            rank = rank + jnp.where(beats, 1.0, 0.0)
        selected = (rank < TOP_K_A) & past
        pen_rows = jnp.where(selected, -(slope * blk) * blocks_back, NEG_INF)
        per_head.append((q_h, pen_rows, slope * dist0))

    for e, extent in cases:
        @pl.when(case == e)
        def _(extent=extent):
            for hh, (q_h, pen_rows, bias0) in enumerate(per_head):
                s = lax.dot_general(kb_ref[0:extent, :], q_h, _NT, preferred_element_type=F32) * scale
                parts = []
                for n in range(extent // blk):
                    pen = jnp.where(i_row == n, causal_pen, pen_rows[n:n + 1, :])
                    parts.append(s[n * blk:(n + 1) * blk, :] - bias0 + pen)
                out = _softmax_av_t(jnp.concatenate(parts, axis=0), vt_ref[0, :, 0:extent])
                o_ref[:, hh * d:(hh + 1) * d] = out.astype(o_ref.dtype)


def _moba_prompt(q2d, k, vt, batch, seq):
    assert seq % BLOCK_A == 0 and Q_TILE == BLOCK_A and N_HEADS % MOBA_HEADS_PER_STEP == 0
    nb = seq // BLOCK_A
    hd = N_HEADS * HEAD_DIM_A
    wq = MOBA_HEADS_PER_STEP * HEAD_DIM_A
    return pl.pallas_call(
        functools.partial(_moba_prompt_kernel, nb=nb, scale=HEAD_DIM_A ** -0.5),
        grid=(batch, nb, N_HEADS // MOBA_HEADS_PER_STEP),
        in_specs=[
            pl.BlockSpec((BLOCK_A, wq), lambda b, i, g: (b * nb + i, g)),
            pl.BlockSpec((1, seq, HEAD_DIM_A), lambda b, i, g: (b, 0, 0)),
            pl.BlockSpec((1, HEAD_DIM_A, seq), lambda b, i, g: (b, 0, 0)),
        ],
        out_specs=pl.BlockSpec((BLOCK_A, wq), lambda b, i, g: (b * nb + i, g)),
        out_shape=jax.ShapeDtypeStruct((batch * seq, hd), BF16),
        scratch_shapes=[
            pltpu.VMEM((seq, HEAD_DIM_A), BF16),
            pltpu.VMEM((nb, HEAD_DIM_A), F32),
        ],
        compiler_params=_params(("parallel", "arbitrary", "arbitrary")),
        name="moba_prompt",
    )(q2d, k, vt)


def _mla_prompt_kernel(qn_ref, qr_ref, kn_ref, kr_ref, vt_ref, o_ref, *, seq, scale):
    i = pl.program_id(2)
    t = Q_TILE
    dn, dv = NOPE_DIM, V_DIM_B
    qr = qr_ref[...]
    tok_pos = i * t + lax.broadcasted_iota(jnp.int32, (1, t), 1)
    cases, case = _extent_cases(i, seq)
    for e, extent in cases:
        @pl.when(case == e)
        def _(extent=extent):
            key_pos = lax.broadcasted_iota(jnp.int32, (extent, t), 0)
            for j in range(MLA_HEADS_PER_STEP):
                q = jnp.concatenate([qn_ref[:, j * dn:(j + 1) * dn], qr], axis=1)
                keys = jnp.concatenate([kn_ref[0:extent, j * dn:(j + 1) * dn], kr_ref[0, j, 0:extent, :]], axis=1)
                s = lax.dot_general(keys, q, _NT, preferred_element_type=F32) * scale
                s = jnp.where(key_pos <= tok_pos, s, NEG_INF)
                out = _softmax_av_t(s, vt_ref[j * dv:(j + 1) * dv, 0:extent])
                o_ref[:, j * dv:(j + 1) * dv] = out.astype(o_ref.dtype)


def _mla_prompt(qn, qr, kn, kr_exp, vt, batch, seq):
    t = Q_TILE
    hs = MLA_HEADS_PER_STEP
    groups_per_lanes = LANES // ROPE_DIM // hs
    assert seq % t == 0 and (LANES // ROPE_DIM) % hs == 0 and N_HEADS % hs == 0
    nq = seq // t
    return pl.pallas_call(
        functools.partial(_mla_prompt_kernel, seq=seq, scale=(NOPE_DIM + ROPE_DIM) ** -0.5),
        grid=(batch, N_HEADS // hs, nq),
        in_specs=[
            pl.BlockSpec((t, hs * NOPE_DIM), lambda b, g, i: (b * nq + i, g)),
            pl.BlockSpec((t, LANES), lambda b, g, i: (b * nq + i, g // groups_per_lanes)),
            pl.BlockSpec((seq, hs * NOPE_DIM), lambda b, g, i: (b, g)),
            pl.BlockSpec((1, hs, seq, LANES), lambda b, g, i: (b, g % groups_per_lanes, 0, 0)),
            pl.BlockSpec((hs * V_DIM_B, seq), lambda b, g, i: (g, b)),
        ],
        out_specs=pl.BlockSpec((t, hs * V_DIM_B), lambda b, g, i: (b * nq + i, g)),
        out_shape=jax.ShapeDtypeStruct((batch * seq, N_HEADS * V_DIM_B), BF16),
        compiler_params=_params(("parallel", "parallel", "arbitrary")),
        name="mla_prompt",
    )(qn, qr, kn, kr_exp, vt)


CHUNK_PAGES = 16
N_SLOTS = 4


def _chunk_copies(page_of, specs, slot):
    out = []
    for p in range(CHUNK_PAGES):
        page = page_of(p)
        for hbm, layer, buf, sem, transposed in specs:
            rows = pl.ds(p * PAGE_SIZE, PAGE_SIZE)
            dst = buf.at[slot, :, rows] if transposed else buf.at[slot, rows]
            out.append(pltpu.make_async_copy(hbm.at[layer, page], dst, sem.at[slot]))
    return out


def _stream_chunks(pt_ref, b, n_seq, nchunks, specs, phase, init):
    ahead = N_SLOTS - 1
    n_groups = nchunks // N_SLOTS

    def start(seq, c, slot):
        for cpy in _chunk_copies(lambda p: pt_ref[seq, c * CHUNK_PAGES + p], specs, slot):
            cpy.start()

    def wait(slot):
        for cpy in _chunk_copies(lambda p: 0, specs, slot):
            cpy.wait()

    @pl.when(b == 0)
    def _():
        for c in range(ahead):
            start(0, c, c)

    def group(k, carry):
        last = k == n_groups - 1
        for j in range(N_SLOTS):
            c = N_SLOTS * k + j
            wait(j)
            if j + ahead < N_SLOTS:
                start(b, c + ahead, j + ahead)
            else:
                in_next = j + ahead - N_SLOTS

                @pl.when(jnp.logical_or(jnp.logical_not(last), b + 1 < n_seq))
                def _(in_next=in_next, c=c):
                    start(jnp.where(last, b + 1, b), jnp.where(last, in_next, c + ahead), in_next)

            carry = phase(c, j, carry)
        return carry

    return lax.fori_loop(0, n_groups, group, init)


def _mla_decode_kernel(pt_ref, ql_ref, qr_ref, cnew_ref, rnew_ref, ckv_hbm, krt_hbm, o_ref,
                       cbuf, rbuf, csem, rsem, kb_ref, p_ref, *, layer, n_pages, n_seq, scale):
    b = pl.program_id(0)
    nchunks = n_pages // CHUNK_PAGES
    specs = [(ckv_hbm, layer, cbuf, csem, False), (krt_hbm, layer, rbuf, rsem, True)]
    ql = ql_ref[0]
    qr = qr_ref[0]

    @pl.when(b == 0)
    def _():
        kb_ref[1] = jnp.zeros(kb_ref.shape[1:], BF16)
        p_ref[1] = jnp.zeros(p_ref.shape[1:], BF16)

    def deferred_av(slot):
        return jnp.dot(p_ref[slot], kb_ref[slot], preferred_element_type=F32)

    def phase(c, slot, carry):
        m, l, acc, a_prev = carry
        half = slot % 2
        acc = acc * a_prev + deferred_av(1 - half)
        kc = cbuf[slot].astype(BF16)
        kb_ref[half] = kc
        kr = rbuf[slot].astype(BF16)
        s = (lax.dot_general(ql, kc, _NT, preferred_element_type=F32)
             + jnp.dot(qr, kr, preferred_element_type=F32)) * scale
        m_new = jnp.maximum(m, jnp.max(s, axis=1, keepdims=True))
        a = jnp.exp(m - m_new)
        p = jnp.exp(s - m_new)
        p_ref[half] = p.astype(BF16)
        return m_new, a * l + jnp.sum(p, axis=1, keepdims=True), acc, a

    init = (jnp.full((N_HEADS, 1), NEG_INF, F32), jnp.zeros((N_HEADS, 1), F32),
            jnp.zeros((N_HEADS, KV_LORA), F32), jnp.zeros((N_HEADS, 1), F32))
    m, l, acc, a_prev = _stream_chunks(pt_ref, b, n_seq, nchunks, specs, phase, init)
    acc = acc * a_prev + deferred_av(1)
    cnew = cnew_ref[0]
    rnew = rnew_ref[0]
    s_new = (jnp.sum(ql.astype(F32) * cnew, axis=1, keepdims=True)
             + jnp.sum(qr.astype(F32) * rnew, axis=1, keepdims=True)) * scale
    m_f = jnp.maximum(m, s_new)
    a = jnp.exp(m - m_f)
    pn = jnp.exp(s_new - m_f)
    o_ref[0] = (a * acc + pn * cnew) / (a * l + pn)


def _mla_decode(page_table, q_lat, q_rope, ckv_new, kr_new, cache_ckv, cache_kr_t, layer):
    db, n_pages = page_table.shape
    assert n_pages % (N_SLOTS * CHUNK_PAGES) == 0 and N_SLOTS % 2 == 0
    rows = CHUNK_PAGES * PAGE_SIZE
    seq3 = lambda b, pt: (b, 0, 0)
    grid_spec = pltpu.PrefetchScalarGridSpec(
        num_scalar_prefetch=1,
        grid=(db,),
        in_specs=[
            pl.BlockSpec((1, N_HEADS, KV_LORA), seq3),
            pl.BlockSpec((1, N_HEADS, ROPE_DIM), seq3),
            pl.BlockSpec((1, 1, KV_LORA), seq3),
            pl.BlockSpec((1, 1, ROPE_DIM), seq3),
            pl.BlockSpec(memory_space=pl.ANY),
            pl.BlockSpec(memory_space=pl.ANY),
        ],
        out_specs=pl.BlockSpec((1, N_HEADS, KV_LORA), seq3),
        scratch_shapes=[
            pltpu.VMEM((N_SLOTS, rows, KV_LORA), F32),
            pltpu.VMEM((N_SLOTS, ROPE_DIM, rows), F32),
            pltpu.SemaphoreType.DMA((N_SLOTS,)),
            pltpu.SemaphoreType.DMA((N_SLOTS,)),
            pltpu.VMEM((2, rows, KV_LORA), BF16),
            pltpu.VMEM((2, N_HEADS, rows), BF16),
        ],
    )
    return pl.pallas_call(
        functools.partial(_mla_decode_kernel, layer=layer, n_pages=n_pages, n_seq=db,
                          scale=(NOPE_DIM + ROPE_DIM) ** -0.5),
        grid_spec=grid_spec,
        out_shape=jax.ShapeDtypeStruct((db, N_HEADS, KV_LORA), F32),
        compiler_params=_params(("arbitrary",)),
        name="mla_decode",
    )(page_table, q_lat, q_rope, ckv_new, kr_new, cache_ckv, cache_kr_t)


def _moba_decode_kernel(pt_ref, q_ref, knew_ref, vnew_ref, k_hbm, v_hbm, o_ref,
                        kbuf, vbuf, ksem, vsem, kmean_ref, m_ref, l_ref, acc_ref, vb_ref, p_ref,
                        *, layer, n_pages, n_seq, scale):
    b = pl.program_id(0)
    nchunks = n_pages // CHUNK_PAGES
    blk = BLOCK_A
    d = HEAD_DIM_A
    rows = CHUNK_PAGES * PAGE_SIZE
    bpc = rows // blk
    nblk = n_pages * PAGE_SIZE // blk
    nblk_pad = kmean_ref.shape[0]
    past = n_pages * PAGE_SIZE
    specs = [(k_hbm, layer, kbuf, ksem, False), (v_hbm, layer, vbuf, vsem, False)]

    q = q_ref[0]
    head = lax.broadcasted_iota(jnp.int32, (N_HEADS, 1), 0).astype(F32)
    slopes = jnp.exp2(-8.0 * (head + 1.0) / N_HEADS)
    key_off = lax.broadcasted_iota(jnp.int32, (1, rows), 1).astype(F32)
    if nblk_pad > nblk:
        kmean_ref[nblk:, :] = jnp.zeros((nblk_pad - nblk, d), F32)

    @pl.when(b == 0)
    def _():
        vb_ref[1] = jnp.zeros(vb_ref.shape[1:], BF16)
        p_ref[1] = jnp.zeros(p_ref.shape[1:], BF16)

    def deferred_av(slot, first_block):
        accs = [jnp.dot(p_ref[slot, :, j * blk:(j + 1) * blk], vb_ref[slot, j * blk:(j + 1) * blk, :],
                        preferred_element_type=F32) for j in range(bpc)]
        acc_ref[pl.ds(first_block, bpc)] = jnp.stack(accs)

    def phase(c, slot, carry):
        half = slot % 2
        deferred_av(1 - half, jnp.where(c == 0, nblk, (c - 1) * bpc))
        kc = kbuf[slot]
        first = pl.multiple_of(c * bpc, bpc)
        kmean_ref[pl.ds(first, bpc), :] = jnp.mean(kc.reshape(bpc, blk, d), axis=1)
        vb_ref[half] = vbuf[slot].astype(BF16)
        dist = lax.convert_element_type(past - c * rows, F32) - key_off
        s = lax.dot_general(q, kc.astype(BF16), _NT, preferred_element_type=F32) * scale - slopes * dist
        ms, ls, ps = [], [], []
        for j in range(bpc):
            sj = s[:, j * blk:(j + 1) * blk]
            mj = jnp.max(sj, axis=1, keepdims=True)
            pj = jnp.exp(sj - mj)
            ps.append(pj.astype(BF16))
            ls.append(jnp.broadcast_to(jnp.sum(pj, axis=1, keepdims=True), (N_HEADS, d)))
            ms.append(jnp.broadcast_to(mj, (N_HEADS, d)))
        p_ref[half] = jnp.concatenate(ps, axis=1)
        m_ref[pl.ds(first, bpc)] = jnp.stack(ms)
        l_ref[pl.ds(first, bpc)] = jnp.stack(ls)
        return carry

    _stream_chunks(pt_ref, b, n_seq, nchunks, specs, phase, 0)
    deferred_av(1, (nchunks - 1) * bpc)

    km_hi, km_mid, km_lo = _split3(kmean_ref[...])
    gate = (lax.dot_general(q, km_hi, _NT, preferred_element_type=F32)
            + lax.dot_general(q, km_mid, _NT, preferred_element_type=F32)
            + lax.dot_general(q, km_lo, _NT, preferred_element_type=F32))
    idx = lax.broadcasted_iota(jnp.int32, (N_HEADS, nblk_pad), 1).astype(F32)
    gate = jnp.where(idx < nblk, gate, NEG_INF)
    sel = jnp.zeros((N_HEADS, nblk_pad), F32)
    for _ in range(min(TOP_K_A, nblk)):
        mx = jnp.max(gate, axis=1, keepdims=True)
        first = jnp.min(jnp.where(gate == mx, idx, float(nblk_pad)), axis=1, keepdims=True)
        pick = idx == first
        sel = jnp.where(pick, 1.0, sel)
        gate = jnp.where(pick, NEG_INF, gate)

    knew = knew_ref[0]
    vnew = vnew_ref[0]
    s_own = jnp.sum(q.astype(F32) * knew, axis=1, keepdims=True) * scale
    m_all = jnp.broadcast_to(s_own, (N_HEADS, d))
    sel_b = []
    for n in range(nblk):
        sn = jnp.broadcast_to(sel[:, n:n + 1], (N_HEADS, d)) > 0.0
        sel_b.append(sn)
        m_all = jnp.maximum(m_all, jnp.where(sn, m_ref[n], NEG_INF))
    w_own = jnp.exp(jnp.broadcast_to(s_own, (N_HEADS, d)) - m_all)
    l_all = w_own
    acc_all = w_own * vnew
    for n in range(nblk):
        w = jnp.where(sel_b[n], jnp.exp(m_ref[n] - m_all), 0.0)
        l_all = l_all + w * l_ref[n]
        acc_all = acc_all + w * acc_ref[n]
    o_ref[0] = acc_all / l_all


def _moba_decode(page_table, q, k_new, v_new, cache_k, cache_v, layer):
    db, n_pages = page_table.shape
    assert n_pages % (N_SLOTS * CHUNK_PAGES) == 0 and N_SLOTS % 2 == 0 and (CHUNK_PAGES * PAGE_SIZE) % BLOCK_A == 0
    rows = CHUNK_PAGES * PAGE_SIZE
    nblk = n_pages * PAGE_SIZE // BLOCK_A
    nblk_pad = -(-nblk // LANES) * LANES
    d = HEAD_DIM_A
    seq3 = lambda b, pt: (b, 0, 0)
    grid_spec = pltpu.PrefetchScalarGridSpec(
        num_scalar_prefetch=1,
        grid=(db,),
        in_specs=[
            pl.BlockSpec((1, N_HEADS, d), seq3),
            pl.BlockSpec((1, 1, d), seq3),
            pl.BlockSpec((1, 1, d), seq3),
            pl.BlockSpec(memory_space=pl.ANY),
            pl.BlockSpec(memory_space=pl.ANY),
        ],
        out_specs=pl.BlockSpec((1, N_HEADS, d), seq3),
        scratch_shapes=[
            pltpu.VMEM((N_SLOTS, rows, d), F32),
            pltpu.VMEM((N_SLOTS, rows, d), F32),
            pltpu.SemaphoreType.DMA((N_SLOTS,)),
            pltpu.SemaphoreType.DMA((N_SLOTS,)),
            pltpu.VMEM((nblk_pad, d), F32),
            pltpu.VMEM((nblk, N_HEADS, d), F32),
            pltpu.VMEM((nblk, N_HEADS, d), F32),
            pltpu.VMEM((nblk + rows // BLOCK_A, N_HEADS, d), F32),
            pltpu.VMEM((2, rows, d), BF16),
            pltpu.VMEM((2, N_HEADS, rows), BF16),
        ],
    )
    return pl.pallas_call(
        functools.partial(_moba_decode_kernel, layer=layer, n_pages=n_pages, n_seq=db,
                          scale=HEAD_DIM_A ** -0.5),
        grid_spec=grid_spec,
        out_shape=jax.ShapeDtypeStruct((db, N_HEADS, d), F32),
        compiler_params=_params(("arbitrary",)),
        name="moba_decode",
    )(page_table, q, k_new, v_new, cache_k, cache_v)


def _rope_tables(pos):
    half = ROPE_DIM // 2
    inv = jnp.power(ROPE_THETA, -2.0 * jnp.arange(half, dtype=F32) / ROPE_DIM)
    ang = pos.astype(F32)[:, None] * inv[None, :]
    cos, sin = jnp.cos(ang), jnp.sin(ang)
    return jnp.concatenate([cos, cos], -1), jnp.concatenate([-sin, sin], -1)


def _layer_weights(l, w_in, w_uq, g_qn, g_kvn, w_uk, w_uv, w_oa, w_ob, w_out, ln1_g, ln1_b,
                   w_up, w_down, ln2_g, ln2_b):
    dm = w_in.shape[1]
    win_t = jnp.swapaxes(w_in[l], 0, 1)
    hd = N_HEADS * HEAD_DIM_A
    o_k, o_cq = hd, hd + 2 * HEAD_DIM_A
    o_ckv = o_cq + Q_LORA
    o_kr = o_ckv + KV_LORA
    o_g = o_kr + ROPE_DIM
    half = ROPE_DIM // 2
    swap = jnp.concatenate([jnp.arange(half, ROPE_DIM), jnp.arange(half)])
    kr_rows = win_t[o_kr:o_g]
    pad = jnp.zeros((LANES - ROPE_DIM, dm), F32)
    w_kv_t = jnp.concatenate([win_t[o_k:o_cq], win_t[o_ckv:o_kr], kr_rows, pad, kr_rows[swap], pad], axis=0)
    wuq = w_uq[l].reshape(Q_LORA, N_HEADS, NOPE_DIM + ROPE_DIM)
    rope_cols = wuq[:, :, NOPE_DIM:]
    w_q = jnp.concatenate([
        wuq[:, :, :NOPE_DIM].reshape(Q_LORA, N_HEADS * NOPE_DIM),
        rope_cols.reshape(Q_LORA, N_HEADS * ROPE_DIM),
        rope_cols[:, :, swap].reshape(Q_LORA, N_HEADS * ROPE_DIM)], axis=1)
    return dict(
        w_qc_t=jnp.concatenate([win_t[:hd], win_t[o_cq:o_ckv]], axis=0).astype(BF16),
        w_kv_t=w_kv_t.astype(BF16),
        w_gate_t=win_t[o_g:].astype(BF16),
        w_q=w_q.astype(BF16),
        w_kup=w_uk[l].reshape(KV_LORA, N_HEADS * NOPE_DIM).astype(BF16),
        w_uv_t=w_uv[l].reshape(KV_LORA, N_HEADS * V_DIM_B).T.astype(BF16),
        w_uk_h=jnp.transpose(w_uk[l], (1, 2, 0)).astype(BF16),
        w_uv_h=jnp.transpose(w_uv[l], (1, 0, 2)).astype(BF16),
        w_oa=w_oa[l].astype(BF16), w_ob=w_ob[l].astype(BF16), w_out=w_out[l].astype(BF16),
        w_up=w_up[l].astype(BF16), w_down=w_down[l].astype(BF16),
        g_qn=g_qn[l][None, :], g_kvn=g_kvn[l][None, :],
        ln1_g=ln1_g[l][None, :], ln1_b=ln1_b[l][None, :], ln2_g=ln2_g[l][None, :], ln2_b=ln2_b[l][None, :],
    )


def _project(xb, w, cos_k, sin_k, cos_q, sin_q):
    qc = _matmul(xb, w["w_qc_t"], out_dtype=BF16, nt=True, name="proj_q")
    k_a, v_a, ckv, krope = _kv_project(xb, w["w_kv_t"], w["g_kvn"], cos_k, sin_k)
    gates = _matmul(xb, w["w_gate_t"], out_dtype=BF16, act="sigmoid", nt=True, name="proj_gates")
    qn, qr = _q_project(qc, w["g_qn"], w["w_q"], cos_q, sin_q)
    return qc, k_a, v_a, ckv, krope, gates, qn, qr


def _finish(x, att_a, att_b, gates, w, alpha):
    mixin = _gated_pair_matmul(att_a, w["w_oa"], att_b, w["w_ob"], gates)
    mix = _matmul(mixin, w["w_out"], out_dtype=F32, name="w_out")
    x1, x1b = _residual_layernorm(x, mix, w["ln1_g"], w["ln1_b"], alpha)
    hdn = _matmul(x1b, w["w_up"], out_dtype=BF16, act="relu2", name="ffn_up")
    ff = _matmul(hdn, w["w_down"], out_dtype=F32, name="ffn_down")
    return _residual_layernorm(x1, ff, w["ln2_g"], w["ln2_b"], alpha)


def kernel(x_prompt, x_sample, cache_moba_k, cache_moba_v, cache_mla_ckv, cache_mla_krope, page_table, w_in, w_uq, g_qn, g_kvn, w_uk, w_uv, w_oa, w_ob, w_out, ln1_g, ln1_b, w_up, w_down, ln2_g, ln2_b):
    depth, dm, _ = w_in.shape
    batch, seq, _ = x_prompt.shape
    db, dec_seq, _ = x_sample.shape
    n_pages = page_table.shape[1]
    past = n_pages * PAGE_SIZE
    assert dec_seq == 1 and past % BLOCK_A == 0 and cache_moba_k.shape[2] == PAGE_SIZE
    alpha = float((2 * depth) ** 0.25)
    hd = N_HEADS * HEAD_DIM_A
    p_rows = batch * seq

    cos_p, sin_p = _rope_tables(jnp.arange(seq))
    cos_s, sin_s = _rope_tables(jnp.full((db,), past))
    cos_kp, sin_kp = jnp.tile(cos_p, (batch, 1)), jnp.tile(sin_p, (batch, 1))
    cos_qp, sin_qp = jnp.tile(cos_p, (1, N_HEADS)), jnp.tile(sin_p, (1, N_HEADS))
    cos_qs, sin_qs = jnp.tile(cos_s, (1, N_HEADS)), jnp.tile(sin_s, (1, N_HEADS))
    lane_groups = LANES // ROPE_DIM
    cache_kr_t = jnp.swapaxes(cache_mla_krope, 2, 3)

    xp = x_prompt.reshape(p_rows, dm)
    xs = x_sample.reshape(db, dm)
    xpb, xsb = xp.astype(BF16), xs.astype(BF16)
    new_p, new_s = [], []
    for l in range(depth):
        w = _layer_weights(l, w_in, w_uq, g_qn, g_kvn, w_uk, w_uv, w_oa, w_ob, w_out, ln1_g, ln1_b,
                           w_up, w_down, ln2_g, ln2_b)
        qc, k_a, v_a, ckv, krope, gates, qn, qr = _project(xpb, w, cos_kp, sin_kp, cos_qp, sin_qp)
        k3 = k_a.reshape(batch, seq, HEAD_DIM_A)
        vt_a = jnp.swapaxes(v_a.reshape(batch, seq, HEAD_DIM_A), 1, 2).astype(BF16)
        att_a = _moba_prompt(qc, k3, vt_a, batch, seq)
        ckv_b = ckv.astype(BF16)
        k_nope = _matmul(ckv_b, w["w_kup"], out_dtype=BF16, name="k_up")
        vt_b = _matmul(w["w_uv_t"], ckv_b, out_dtype=BF16, nt=True, name="v_up_t")
        kr3 = krope.reshape(batch, seq, ROPE_DIM).astype(BF16)
        kr_exp = jnp.stack([jnp.pad(kr3, ((0, 0), (0, 0), (g * ROPE_DIM, LANES - (g + 1) * ROPE_DIM)))
                            for g in range(lane_groups)], axis=1)
        att_b = _mla_prompt(qn, qr, k_nope, kr_exp, vt_b, batch, seq)
        xp, xpb = _finish(xp, att_a, att_b, gates, w, alpha)
        new_p.append((k3, v_a.reshape(batch, seq, -1), ckv.reshape(batch, seq, -1), krope.reshape(batch, seq, -1)))

        qc, k_a, v_a, ckv, krope, gates, qn, qr = _project(xsb, w, cos_s, sin_s, cos_qs, sin_qs)
        q_a = qc[:, :hd].reshape(db, N_HEADS, HEAD_DIM_A)
        att_a = _moba_decode(page_table, q_a, k_a.reshape(db, 1, -1), v_a.reshape(db, 1, -1),
                             cache_moba_k, cache_moba_v, l)
        q_lat = _head_matmul(qn, w["w_uk_h"], BF16).reshape(db, N_HEADS, KV_LORA)
        att_lat = _mla_decode(page_table, q_lat, qr.reshape(db, N_HEADS, ROPE_DIM),
                              ckv.reshape(db, 1, -1), krope.reshape(db, 1, -1),
                              cache_mla_ckv, cache_kr_t, l)
        att_b = _head_matmul(att_lat.reshape(db, N_HEADS * KV_LORA), w["w_uv_h"], BF16)
        xs, xsb = _finish(xs, att_a.reshape(db, hd).astype(BF16), att_b, gates, w, alpha)
        new_s.append((k_a.reshape(db, 1, -1), v_a.reshape(db, 1, -1),
                      ckv.reshape(db, 1, -1), krope.reshape(db, 1, -1)))

    stack = lambda items, j: jnp.stack([it[j] for it in items])
    return (xp.reshape(batch, seq, dm), xs.reshape(db, dec_seq, dm),
            stack(new_p, 0), stack(new_p, 1), stack(new_p, 2), stack(new_p, 3),
            stack(new_s, 0), stack(new_s, 1), stack(new_s, 2), stack(new_s, 3))
```
